```python
import math
import jax, jax.numpy as jnp
from jax import lax
import numpy as np

D_MODEL = 1024
BATCH = 16
SEQ = 4096
DEPTH = 1
DEC_BATCH = 128
DEC_SEQ = 4
PAST_LEN = 8192
PAGE_SIZE = 128

SB_HEADS = 8
SB_DH = 64
SB_W = SB_HEADS * SB_DH
SB_BLOCK = 128
SB_BIAS_INIT = -7.0
HG_HEADS = 4
HG_DK = 128
HG_DV = 128
HG_W = HG_HEADS * HG_DK
HG_CHUNK = 64
IN_WIDTH = 3 * SB_W + 4 * HG_W
N_MEM = 256
MEM_HEADS = 4
MEM_DH = D_MODEL // MEM_HEADS
N_EXPERTS = 32
TOP_K = 4
D_FF = D_MODEL
MOE_BLOCK = 128
SWIGLU_ALPHA = 1.702
SWIGLU_LIMIT = 7.0
ALPHA = (2 * DEPTH) ** 0.25
BETA = (8 * DEPTH) ** -0.25
LN_EPS = 1e-5
RMS_EPS = 1e-6

kernel_name = 'hybrid_stickbreak_hgrn2_moe_step'

F32 = jnp.float32


def layer_norm(x, g, b):
    xf = x.astype(F32)
    mu = jnp.mean(xf, axis=-1, keepdims=True)
    var = jnp.mean(jnp.square(xf - mu), axis=-1, keepdims=True)
    return ((xf - mu) * lax.rsqrt(var + LN_EPS) * g + b).astype(x.dtype)


def stick_breaking_attention(q, k_past, v_past, k_new, v_new, bias):
    B, Lq, H, dh = q.shape
    P = k_past.shape[1]
    Lk = P + Lq
    blk = math.gcd(Lq, SB_BLOCK)
    nb = Lq // blk
    k_pos = jnp.arange(Lk)
    q_blocks = q.reshape(B, nb, blk, H, dh).swapaxes(0, 1)
    q_pos = (P + jnp.arange(Lq)).reshape(nb, blk)
    bias_f = bias.astype(F32)[None, :, None, None]

    def one_block(args):
        qi, pi = args
        z = jnp.concatenate([jnp.einsum('bqhd,bkhd->bhqk', qi, k_past),
                             jnp.einsum('bqhd,bkhd->bhqk', qi, k_new)], axis=-1).astype(F32) + bias_f
        causal = k_pos[None, :] < pi[:, None]
        log_1m_beta = jnp.where(causal, jax.nn.log_sigmoid(-z), 0.0)
        suffix = lax.cumsum(log_1m_beta, axis=3, reverse=True)
        log_a = jax.nn.log_sigmoid(z) + suffix - log_1m_beta
        a = jnp.where(causal, jnp.exp(log_a), 0.0).astype(v_new.dtype)
        return (jnp.einsum('bhqk,bkhd->bqhd', a[..., :P], v_past)
                + jnp.einsum('bhqk,bkhd->bqhd', a[..., P:], v_new))

    out = lax.map(one_block, (q_blocks, q_pos))
    return out.swapaxes(0, 1).reshape(B, Lq, H, dh)


def hgrn2_recurrence(q, k, v, g, s0):
    B, L, H, _ = q.shape
    DV = v.shape[-1]
    C = math.gcd(L, HG_CHUNK)
    n = L // C
    to_chunks = lambda t: t.reshape(B, n, C, H, t.shape[-1]).transpose(1, 0, 3, 2, 4)
    causal = jnp.tril(jnp.ones((C, C), dtype=bool))[:, :, None]

    def step(S, inp):
        qc, kc, vc, gc = inp
        b = jnp.cumsum(gc, axis=2)
        o_inter = jnp.einsum('bhtk,bhkv->bhtv', qc * jnp.exp(b), S)
        rel = jnp.where(causal, b[:, :, :, None, :] - b[:, :, None, :, :], -jnp.inf)
        att = jnp.einsum('bhtk,bhtsk,bhsk->bhts', qc, jnp.exp(rel), kc)
        o_intra = jnp.einsum('bhts,bhsv->bhtv', att, vc)
        b_end = b[:, :, -1:, :]
        S = (jnp.exp(b_end[:, :, 0, :])[..., None] * S
             + jnp.einsum('bhsk,bhsv->bhkv', kc * jnp.exp(b_end - b), vc))
        return S, o_inter + o_intra

    s_fin, o = lax.scan(step, s0, (to_chunks(q), to_chunks(k), to_chunks(v), to_chunks(g)))
    return o.transpose(1, 0, 3, 2, 4).reshape(B, L, H, DV), s_fin


def hybrid_mixer(x, sb_k_past, sb_v_past, hg_s0, lb, w_in, sb_bias, hg_norm_g, w_out):
    B, L, _ = x.shape
    bounds = [int(c) for c in np.cumsum([SB_W] * 3 + [HG_W] * 4)[:-1]]
    q_sb, k_sb, v_sb, q_hg, f_hg, i_hg, o_hg = jnp.split(x @ w_in, bounds, axis=-1)
    q_sb = q_sb.reshape(B, L, SB_HEADS, SB_DH) * (SB_DH ** -0.5)
    k_sb = k_sb.reshape(B, L, SB_HEADS, SB_DH)
    v_sb = v_sb.reshape(B, L, SB_HEADS, SB_DH)
    o_sb = stick_breaking_attention(q_sb, sb_k_past, sb_v_past, k_sb, v_sb, sb_bias)
    f = lb + (1.0 - lb) * jax.nn.sigmoid(f_hg.astype(F32))
    hh = lambda t: t.reshape(B, L, HG_HEADS, -1)
    o_h, s_new = hgrn2_recurrence(hh(jax.nn.silu(q_hg.astype(F32))), hh(1.0 - f),
                                  hh(i_hg.astype(F32)), hh(jnp.log(f)), hg_s0.astype(F32))
    o_h = o_h * lax.rsqrt(jnp.mean(jnp.square(o_h), axis=-1, keepdims=True) + RMS_EPS)
    o_h = o_h.reshape(B, L, HG_W) * hg_norm_g * jax.nn.sigmoid(o_hg.astype(F32))
    merged = jnp.concatenate([o_sb.reshape(B, L, SB_W), o_h.astype(x.dtype)], axis=-1)
    return merged @ w_out, k_sb, v_sb, s_new.astype(x.dtype)


def mem_cross_attention(h, mem_k, mem_v, w_mq, w_mo):
    B, L, D = h.shape
    q = (h @ w_mq).reshape(B, L, MEM_HEADS, MEM_DH) * (MEM_DH ** -0.5)
    s = jnp.einsum('blhd,bmhd->bhlm', q, mem_k).astype(F32)
    p = jax.nn.softmax(s, axis=-1).astype(h.dtype)
    o = jnp.einsum('bhlm,bmhd->blhd', p, mem_v).reshape(B, L, D)
    return o @ w_mo


def moe(x2, w_router, b_router, w_gate_up, b_gate_up, w_down, b_down):
    T, D = x2.shape
    logits = (x2 @ w_router + b_router).astype(F32)
    top_val, top_idx = lax.top_k(logits, TOP_K)
    gates = jax.nn.softmax(top_val, axis=-1)
    A = T * TOP_K
    e_flat = top_idx.reshape(A)
    order = jnp.argsort(e_flat)
    e_sorted = e_flat[order]
    counts = jnp.bincount(e_flat, length=N_EXPERTS)
    start = jnp.cumsum(counts) - counts
    padded = (counts + MOE_BLOCK - 1) // MOE_BLOCK * MOE_BLOCK
    pad_end = jnp.cumsum(padded)
    pad_start = pad_end - padded
    slot_sorted = (pad_start[e_sorted] + jnp.arange(A) - start[e_sorted]).astype(jnp.int32)
    slot = jnp.zeros((A,), jnp.int32).at[order].set(slot_sorted)
    n_blocks = -(-A // MOE_BLOCK) + N_EXPERTS
    n_rows = n_blocks * MOE_BLOCK
    tok_of_slot = jnp.zeros((n_rows,), jnp.int32).at[slot].set(jnp.arange(A, dtype=jnp.int32) // TOP_K)
    block_e = jnp.minimum(jnp.searchsorted(pad_end, jnp.arange(n_blocks) * MOE_BLOCK, side='right'),
                          N_EXPERTS - 1)
    x_blocks = x2[tok_of_slot].reshape(n_blocks, MOE_BLOCK, D)

    def expert_block(args):
        xb, e = args
        hgu = xb @ w_gate_up[e] + b_gate_up[e]
        gate, lin = jnp.split(hgu, 2, axis=-1)
        gate = jnp.minimum(gate, SWIGLU_LIMIT)
        lin = jnp.clip(lin, -SWIGLU_LIMIT, SWIGLU_LIMIT)
        act = gate * jax.nn.sigmoid(SWIGLU_ALPHA * gate) * (lin + 1.0)
        return act @ w_down[e] + b_down[e]

    out = lax.map(expert_block, (x_blocks, block_e)).reshape(n_rows, D)
    return jnp.einsum('tk,tkd->td', gates.astype(x2.dtype), out[slot].reshape(T, TOP_K, D))


def decoder_layer(x, sb_k_past, sb_v_past, hg_s0, mem_k, mem_v, lb, w_in, sb_bias, hg_norm_g, w_out,
                  ln1_g, ln1_b, w_mq, w_mo, ln2_g, ln2_b, w_router, b_router, w_gate_up,
                  b_gate_up, w_down, b_down, ln3_g, ln3_b):
    B, L, D = x.shape
    mix, k_new, v_new, s_new = hybrid_mixer(x, sb_k_past, sb_v_past, hg_s0, lb, w_in, sb_bias,
                                            hg_norm_g, w_out)
    h = layer_norm(ALPHA * x + mix, ln1_g, ln1_b)
    h = layer_norm(ALPHA * h + mem_cross_attention(h, mem_k, mem_v, w_mq, w_mo), ln2_g, ln2_b)
    ff = moe(h.reshape(B * L, D), w_router, b_router, w_gate_up, b_gate_up, w_down, b_down)
    h = layer_norm(ALPHA * h + ff.reshape(B, L, D), ln3_g, ln3_b)
    return h, k_new, v_new, s_new


def setup_inputs(seed: int = 0) -> dict:
    key = jax.random.key(seed)
    keys = iter(jax.random.split(key, 40))

    def nrm(shape, scale=1.0):
        return jax.random.normal(next(keys), shape, F32) * scale

    n_pages = PAST_LEN // PAGE_SIZE
    n_used = DEC_BATCH * n_pages
    n_pool = n_used + n_used // 4
    page_table = jax.random.permutation(next(keys), n_pool)[:n_used].reshape(DEC_BATCH, n_pages).astype(jnp.int32)
    col_scale = jnp.asarray(np.concatenate([
        np.ones(2 * SB_W), np.full(SB_W, BETA), np.ones(2 * HG_W), np.full(HG_W, BETA), np.ones(HG_W)
    ]).astype(np.float32))
    fan = D_MODEL ** -0.5
    gain = lambda: 1.0 + nrm((DEPTH, D_MODEL), 0.02)
    return {
        'x_prompt': nrm((BATCH, SEQ, D_MODEL)),
        'x_sample': nrm((DEC_BATCH, DEC_SEQ, D_MODEL)),
        'mem_prompt': nrm((BATCH, N_MEM, D_MODEL)),
        'cache_sb_k': nrm((DEPTH, n_pool, PAGE_SIZE, SB_HEADS, SB_DH)),
        'cache_sb_v': nrm((DEPTH, n_pool, PAGE_SIZE, SB_HEADS, SB_DH), BETA),
        'state_hgrn': nrm((DEPTH, DEC_BATCH, HG_HEADS, HG_DK, HG_DV), 0.5 * BETA),
        'cache_mem_k': nrm((DEPTH, DEC_BATCH, N_MEM, MEM_HEADS, MEM_DH)),
        'cache_mem_v': nrm((DEPTH, DEC_BATCH, N_MEM, MEM_HEADS, MEM_DH), BETA),
        'page_table': page_table,
        'w_in': nrm((DEPTH, D_MODEL, IN_WIDTH), fan) * col_scale,
        'sb_bias': SB_BIAS_INIT + nrm((DEPTH, SB_HEADS), 0.5),
        'hg_lb_logits': nrm((DEPTH + 1, HG_W), 0.5),
        'hg_norm_g': 1.0 + nrm((DEPTH, HG_W), 0.02),
        'w_out': nrm((DEPTH, D_MODEL, D_MODEL), fan * BETA),
        'ln1_g': gain(),
        'ln1_b': nrm((DEPTH, D_MODEL), 0.02),
        'w_mq': nrm((DEPTH, D_MODEL, D_MODEL), fan),
        'w_mk': nrm((DEPTH, D_MODEL, D_MODEL), fan),
        'w_mv': nrm((DEPTH, D_MODEL, D_MODEL), fan * BETA),
        'w_mo': nrm((DEPTH, D_MODEL, D_MODEL), fan * BETA),
        'ln2_g': gain(),
        'ln2_b': nrm((DEPTH, D_MODEL), 0.02),
        'w_router': nrm((DEPTH, D_MODEL, N_EXPERTS), fan),
        'b_router': nrm((DEPTH, N_EXPERTS), 0.01),
        'w_gate_up': nrm((DEPTH, N_EXPERTS, D_MODEL, 2 * D_FF), fan * BETA),
        'b_gate_up': nrm((DEPTH, N_EXPERTS, 2 * D_FF), 0.02),
        'w_down': nrm((DEPTH, N_EXPERTS, D_FF, D_MODEL), D_FF ** -0.5 * BETA),
        'b_down': nrm((DEPTH, N_EXPERTS, D_MODEL), 0.02),
        'ln3_g': gain(),
        'ln3_b': nrm((DEPTH, D_MODEL), 0.02),
    }


def reference(x_prompt, x_sample, mem_prompt, cache_sb_k, cache_sb_v, state_hgrn, cache_mem_k,
              cache_mem_v, page_table, w_in, sb_bias, hg_lb_logits, hg_norm_g, w_out, ln1_g, ln1_b,
              w_mq, w_mk, w_mv, w_mo, ln2_g, ln2_b, w_router, b_router, w_gate_up, b_gate_up, w_down,
              b_down, ln3_g, ln3_b):
    B = x_prompt.shape[0]
    DB = x_sample.shape[0]
    M = mem_prompt.shape[1]
    past_len = page_table.shape[1] * cache_sb_k.shape[2]
    lb_all = jnp.cumsum(jax.nn.softmax(hg_lb_logits.astype(F32), axis=0), axis=0)
    empty = jnp.zeros((B, 0, SB_HEADS, SB_DH), x_prompt.dtype)
    s0_prompt = jnp.zeros((B, HG_HEADS, HG_DK, HG_DV), F32)
    hp, hs = x_prompt, x_sample
    kp_l, vp_l, sp_l, mk_l, mv_l, ks_l, vs_l, ss_l = [], [], [], [], [], [], [], []
    for l in range(DEPTH):
        lw = (w_in[l], sb_bias[l], hg_norm_g[l], w_out[l], ln1_g[l], ln1_b[l], w_mq[l], w_mo[l], ln2_g[l],
              ln2_b[l], w_router[l], b_router[l], w_gate_up[l], b_gate_up[l], w_down[l], b_down[l],
              ln3_g[l], ln3_b[l])
        mk = (mem_prompt @ w_mk[l]).reshape(B, M, MEM_HEADS, MEM_DH)
        mv = (mem_prompt @ w_mv[l]).reshape(B, M, MEM_HEADS, MEM_DH)
        hp, kp, vp, sp = decoder_layer(hp, empty, empty, s0_prompt, mk, mv, lb_all[l], *lw)
        k_past = cache_sb_k[l][page_table].reshape(DB, past_len, SB_HEADS, SB_DH)
        v_past = cache_sb_v[l][page_table].reshape(DB, past_len, SB_HEADS, SB_DH)
        hs, ks, vs, ss = decoder_layer(hs, k_past, v_past, state_hgrn[l], cache_mem_k[l], cache_mem_v[l],
                                       lb_all[l], *lw)
        kp_l.append(kp); vp_l.append(vp); sp_l.append(sp); mk_l.append(mk); mv_l.append(mv)
        ks_l.append(ks); vs_l.append(vs); ss_l.append(ss)
    return (hp, hs, jnp.stack(kp_l), jnp.stack(vp_l), jnp.stack(sp_l), jnp.stack(mk_l), jnp.stack(mv_l),
            jnp.stack(ks_l), jnp.stack(vs_l), jnp.stack(ss_l))
```

```python
import functools
import math

import numpy as np
import jax
import jax.numpy as jnp
from jax import lax
from jax.experimental import pallas as pl
from jax.experimental.pallas import tpu as pltpu

F32 = jnp.float32
BF16 = jnp.bfloat16

D_MODEL = 1024
SB_HEADS = 8
SB_DH = 64
SB_W = SB_HEADS * SB_DH
HG_HEADS = 4
HG_DK = 128
HG_W = HG_HEADS * HG_DK
MEM_HEADS = 4
MEM_DH = D_MODEL // MEM_HEADS
N_EXPERTS = 32
TOP_K = 4
SWIGLU_ALPHA = 1.702
SWIGLU_LIMIT = 7.0
DEPTH = 1
ALPHA = (2 * DEPTH) ** 0.25
LN_EPS = 1e-5
RMS_EPS = 1e-6

LANES = 128
VMEM_LIMIT = 52 * 1024 * 1024
SB_TQ = 256
SB_TK = 128
HG_CHUNK = 128
HG_SUB = 16
MOE_BM = 256
ROW_PAD = 16


def _cparams(sem):
    return pltpu.CompilerParams(dimension_semantics=sem, vmem_limit_bytes=VMEM_LIMIT)


def _dot(a, b, **kw):
    return jnp.dot(a, b, preferred_element_type=F32, **kw)


def _dot_nt(a, b):
    return lax.dot_general(a, b, (((1,), (1,)), ((), ())), preferred_element_type=F32)


def _layer_norm(x, g, b):
    mu = jnp.mean(x, axis=-1, keepdims=True)
    xc = x - mu
    var = jnp.mean(xc * xc, axis=-1, keepdims=True)
    return xc * lax.rsqrt(var + LN_EPS) * g + b


def _inproj_kernel(x_ref, w_ref, q_ref, k_ref, v_ref, kb_ref, vb_ref, hg_ref):
    xb = x_ref[...].astype(BF16)
    q_ref[...] = _dot(xb, w_ref[:, 0:SB_W]).astype(BF16)
    k = _dot(xb, w_ref[:, SB_W:2 * SB_W])
    k_ref[...] = k
    kb_ref[...] = k.astype(BF16)
    v = _dot(xb, w_ref[:, 2 * SB_W:3 * SB_W])
    v_ref[...] = v
    vb_ref[...] = v.astype(BF16)
    hg_ref[...] = _dot(xb, w_ref[:, 3 * SB_W:])


def _inproj(x2, w_in_b):
    T = x2.shape[0]
    tm = min(256, T)
    n_in = w_in_b.shape[1]
    row = lambda i: (i, 0)
    return pl.pallas_call(
        _inproj_kernel,
        grid=(T // tm,),
        in_specs=[pl.BlockSpec((tm, D_MODEL), row),
                  pl.BlockSpec((D_MODEL, n_in), lambda i: (0, 0))],
        out_specs=[pl.BlockSpec((tm, SB_W), row), pl.BlockSpec((tm, SB_W), row),
                   pl.BlockSpec((tm, SB_W), row), pl.BlockSpec((tm, SB_W), row),
                   pl.BlockSpec((tm, SB_W), row), pl.BlockSpec((tm, 4 * HG_W), row)],
        out_shape=[jax.ShapeDtypeStruct((T, SB_W), BF16), jax.ShapeDtypeStruct((T, SB_W), F32),
                   jax.ShapeDtypeStruct((T, SB_W), F32), jax.ShapeDtypeStruct((T, SB_W), BF16),
                   jax.ShapeDtypeStruct((T, SB_W), BF16), jax.ShapeDtypeStruct((T, 4 * HG_W), F32)],
        compiler_params=_cparams(("parallel",)),
        name="inproj",
    )(x2, w_in_b)


def _memkv_kernel(m_ref, wk_ref, wv_ref, k_ref, v_ref, kb_ref, vb_ref):
    mb = m_ref[...].astype(BF16)
    k = _dot(mb, wk_ref[...])
    v = _dot(mb, wv_ref[...])
    k_ref[...] = k
    v_ref[...] = v
    kb_ref[...] = k.astype(BF16)
    vb_ref[...] = v.astype(BF16)


def _memkv(mem2, wk_b, wv_b):
    T = mem2.shape[0]
    tm = min(256, T)
    row = lambda i: (i, 0)
    full = lambda i: (0, 0)
    sds = lambda dt: jax.ShapeDtypeStruct((T, D_MODEL), dt)
    return pl.pallas_call(
        _memkv_kernel,
        grid=(T // tm,),
        in_specs=[pl.BlockSpec((tm, D_MODEL), row), pl.BlockSpec((D_MODEL, D_MODEL), full),
                  pl.BlockSpec((D_MODEL, D_MODEL), full)],
        out_specs=[pl.BlockSpec((tm, D_MODEL), row)] * 4,
        out_shape=[sds(F32), sds(F32), sds(BF16), sds(BF16)],
        compiler_params=_cparams(("parallel",)),
        name="memkv",
    )(mem2, wk_b, wv_b)


def _sb_tile(z, vb, ue, carry, mask):
    sp = jnp.maximum(z, 0.0) + jnp.log(1.0 + jnp.exp(-jnp.abs(z)))
    l1m = -sp
    if mask is not None:
        l1m = jnp.where(mask, l1m, 0.0)
    cs = _dot(l1m.astype(BF16), ue)
    tk = z.shape[1]
    log_a = (z - sp) + cs[:, :tk] + carry
    a = jnp.exp(log_a)
    if mask is not None:
        a = jnp.where(mask, a, 0.0)
    return _dot(a.astype(BF16), vb), carry + cs[:, tk:]


def _sb_ue(tk):
    j = np.arange(tk)[:, None]
    s = np.arange(tk)[None, :]
    u = (j > s).astype(np.float32)
    return jnp.asarray(np.concatenate([u, np.ones((tk, tk), np.float32)], axis=1), dtype=BF16)


def _sb_prompt_kernel(bias_ref, q_ref, k_ref, v_ref, ue_ref, o_ref, carry_ref, acc_ref, *, tq, tk):
    hp = pl.program_id(1)
    i = pl.program_id(2)
    ue = ue_ref[...]
    rpb = tq // tk
    carry_ref[...] = jnp.zeros_like(carry_ref)
    acc_ref[...] = jnp.zeros_like(acc_ref)
    dcol = (lax.broadcasted_iota(jnp.int32, (tq, tk), 1)
            - lax.broadcasted_iota(jnp.int32, (tq, tk), 0))

    def step(j, mask):
        ks = pl.multiple_of(j * tk, tk)
        for h in range(2):
            sl = slice(h * SB_DH, (h + 1) * SB_DH)
            z = _dot_nt(q_ref[:, sl], k_ref[pl.ds(ks, tk), sl]) + bias_ref[hp * 2 + h]
            pv, c_new = _sb_tile(z, v_ref[pl.ds(ks, tk), sl], ue, carry_ref[h], mask)
            acc_ref[h] += pv
            carry_ref[h] = c_new

    for jj in range(rpb):
        j = (i + 1) * rpb - 1 - jj
        step(j, dcol < (i * tq - j * tk))

    def body(n, c):
        step(i * rpb - 1 - n, None)
        return c

    lax.fori_loop(0, i * rpb, body, 0)
    for h in range(2):
        o_ref[:, h * SB_DH:(h + 1) * SB_DH] = acc_ref[h].astype(BF16)


def _sb_prompt(q, kb, vb, sb_bias, B, L):
    tq = min(SB_TQ, L)
    tk = SB_TK
    nq = L // tq
    nhp = SB_HEADS // 2
    kernel = functools.partial(_sb_prompt_kernel, tq=tq, tk=tk)
    return pl.pallas_call(
        kernel,
        grid_spec=pltpu.PrefetchScalarGridSpec(
            num_scalar_prefetch=1,
            grid=(B, nhp, nq),
            in_specs=[pl.BlockSpec((tq, 2 * SB_DH), lambda b, hp, i, s: (b * nq + i, hp)),
                      pl.BlockSpec((L, 2 * SB_DH), lambda b, hp, i, s: (b, hp)),
                      pl.BlockSpec((L, 2 * SB_DH), lambda b, hp, i, s: (b, hp)),
                      pl.BlockSpec((tk, 2 * tk), lambda b, hp, i, s: (0, 0))],
            out_specs=pl.BlockSpec((tq, 2 * SB_DH), lambda b, hp, i, s: (b * nq + i, hp)),
            scratch_shapes=[pltpu.VMEM((2, tq, tk), F32), pltpu.VMEM((2, tq, SB_DH), F32)]),
        out_shape=jax.ShapeDtypeStruct((B * L, SB_W), BF16),
        compiler_params=_cparams(("parallel", "parallel", "arbitrary")),
        name="sb_prompt",
    )(sb_bias, q, kb, vb, _sb_ue(tk))


def _sb_sample_kernel(pt_ref, qbd_ref, bias_ref, kn_ref, vn_ref, ue_ref, *rest, npg, lq):
    k_refs = rest[:npg]
    v_refs = rest[npg:2 * npg]
    o_ref = rest[2 * npg]
    carry_ref, acc_ref = rest[2 * npg + 1:]
    j = pl.program_id(1)
    rows = lq * SB_HEADS
    qbd = qbd_ref[0]
    bias = bias_ref[...]
    ue = ue_ref[...]

    def block(kb, vb, mask):
        z = _dot_nt(qbd, kb) + bias
        pv, c_new = _sb_tile(z, vb, ue, carry_ref[...], mask)
        acc_ref[...] += pv
        carry_ref[...] = c_new

    @pl.when(j == 0)
    def _():
        carry_ref[...] = jnp.zeros_like(carry_ref)
        acc_ref[...] = jnp.zeros_like(acc_ref)
        col = lax.broadcasted_iota(jnp.int32, (rows, SB_TK), 1)
        t = lax.broadcasted_iota(jnp.int32, (rows, SB_TK), 0) // SB_HEADS
        block(kn_ref[0], vn_ref[0], col < t)

    for p in range(npg):
        block(k_refs[p][0].astype(BF16), v_refs[p][0].astype(BF16), None)

    @pl.when(j == pl.num_programs(1) - 1)
    def _():
        r_h = lax.broadcasted_iota(jnp.int32, (rows, SB_W), 0) % SB_HEADS
        c_h = lax.broadcasted_iota(jnp.int32, (rows, SB_W), 1) // SB_DH
        own = jnp.where(r_h == c_h, acc_ref[...], 0.0)
        o_ref[0] = jnp.sum(own.reshape(lq, SB_HEADS, SB_W), axis=1)


def _sb_sample(q_s, kb_s, vb_s, cache_k, cache_v, page_table, sb_bias, DB, lq):
    n_pool, page = cache_k.shape[0], cache_k.shape[1]
    n_pages = page_table.shape[1]
    npg = math.gcd(n_pages, 8)
    rows = lq * SB_HEADS
    eye = jnp.eye(SB_HEADS, dtype=BF16)
    qbd = (q_s.reshape(DB, lq, SB_HEADS, 1, SB_DH) * eye[None, None, :, :, None]).reshape(DB, rows, SB_W)
    pad_new = lambda a: jnp.pad(a.reshape(DB, lq, SB_W), ((0, 0), (0, page - lq), (0, 0)))
    bias_rows = jnp.broadcast_to(jnp.tile(sb_bias.astype(F32), lq)[:, None], (rows, SB_TK))

    def page_spec(p):
        return pl.BlockSpec((1, page, SB_W),
                            lambda b, j, pt: (pt[b, n_pages - 1 - (j * npg + p)], 0, 0))

    per_b = lambda b, j, pt: (b, 0, 0)
    const = lambda b, j, pt: (0, 0)
    kernel = functools.partial(_sb_sample_kernel, npg=npg, lq=lq)
    return pl.pallas_call(
        kernel,
        grid_spec=pltpu.PrefetchScalarGridSpec(
            num_scalar_prefetch=1,
            grid=(DB, n_pages // npg),
            in_specs=[pl.BlockSpec((1, rows, SB_W), per_b),
                      pl.BlockSpec((rows, SB_TK), const),
                      pl.BlockSpec((1, page, SB_W), per_b),
                      pl.BlockSpec((1, page, SB_W), per_b),
                      pl.BlockSpec((SB_TK, 2 * SB_TK), const)]
                     + [page_spec(p) for p in range(npg)] * 2,
            out_specs=pl.BlockSpec((1, lq, SB_W), per_b),
            scratch_shapes=[pltpu.VMEM((rows, SB_TK), F32), pltpu.VMEM((rows, SB_W), F32)]),
        out_shape=jax.ShapeDtypeStruct((DB, lq, SB_W), F32),
        compiler_params=_cparams(("parallel", "arbitrary")),
        name="sb_sample",
    )(page_table, qbd, bias_rows, pad_new(kb_s), pad_new(vb_s), _sb_ue(SB_TK),
      *([cache_k] * npg), *([cache_v] * npg))


def _hgrn_kernel(*refs, C, c, n_valid, has_s0):
    if has_s0:
        q_ref, f_ref, i_ref, og_ref, lb_ref, gn_ref, ltri_ref, s0_ref, o_ref, sout_ref, S_ref = refs
    else:
        q_ref, f_ref, i_ref, og_ref, lb_ref, gn_ref, ltri_ref, o_ref, sout_ref, S_ref = refs
    ci = pl.program_id(2)

    @pl.when(ci == 0)
    def _():
        S_ref[...] = s0_ref[0, 0] if has_s0 else jnp.zeros_like(S_ref)

    lb = lb_ref[...]
    qh = q_ref[...]
    qh = qh * jax.nn.sigmoid(qh)
    f = lb + (1.0 - lb) * jax.nn.sigmoid(f_ref[...])
    g = jnp.log(f)
    kk = 1.0 - f
    v = i_ref[...]
    if n_valid < C:
        valid = lax.broadcasted_iota(jnp.int32, (C, HG_DK), 0) < n_valid
        g = jnp.where(valid, g, 0.0)
        kk = jnp.where(valid, kk, 0.0)
    b = _dot(ltri_ref[...], g, precision=lax.Precision.HIGHEST)
    S = S_ref[...]
    o = _dot((qh * jnp.exp(b)).astype(BF16), S.astype(BF16))
    vb = v.astype(BF16)

    rr = lax.broadcasted_iota(jnp.int32, (c * c, HG_DK), 0)
    diag_mask = (rr % c) <= (rr // c)
    rep = lambda a: jnp.broadcast_to(a[:, None, :], (c, c, HG_DK)).reshape(c * c, HG_DK)
    til = lambda a: jnp.broadcast_to(a[None, :, :], (c, c, HG_DK)).reshape(c * c, HG_DK)
    parts = []
    for blk in range(C // c):
        r0 = blk * c
        b_i = b[r0:r0 + c]
        q_i = qh[r0:r0 + c]
        x = jnp.exp(jnp.where(diag_mask, rep(b_i) - til(b_i), -1e30)) * rep(q_i) * til(kk[r0:r0 + c])
        att = jnp.sum(x, axis=-1, keepdims=True)
        o_i = jnp.sum((att * til(v[r0:r0 + c])).reshape(c, c, HG_DK), axis=1)
        if blk > 0:
            ref_b = b[r0 - 1:r0]
            qe = q_i * jnp.exp(b_i - ref_b)
            ke = kk[:r0] * jnp.exp(ref_b - b[:r0])
            a_off = _dot_nt(qe.astype(BF16), ke.astype(BF16))
            o_i = o_i + _dot(a_off.astype(BF16), vb[:r0])
        parts.append(o_i)
    o = o + jnp.concatenate(parts, axis=0)

    b_end = b[C - 1:C]
    kd = kk * jnp.exp(b_end - b)
    decay_col = jnp.transpose(jnp.broadcast_to(jnp.exp(b_end), (HG_DK, HG_DK)))
    s_new = decay_col * S + _dot(jnp.transpose(kd).astype(BF16), vb)
    S_ref[...] = s_new
    sout_ref[0, 0] = s_new

    o = o * lax.rsqrt(jnp.mean(o * o, axis=-1, keepdims=True) + RMS_EPS)
    o_ref[...] = (o * gn_ref[...] * jax.nn.sigmoid(og_ref[...])).astype(BF16)


def _hgrn(hg, lb, gn, s0, B, L, n_valid):
    C = HG_CHUNK
    nc = L // C
    has_s0 = s0 is not None
    ltri = jnp.asarray(np.tril(np.ones((C, C), np.float32)))
    tok = lambda grp: pl.BlockSpec((C, HG_DK), lambda b, h, ci: (b * nc + ci, grp * HG_HEADS + h))
    per_h = pl.BlockSpec((1, HG_DK), lambda b, h, ci: (0, h))
    st = pl.BlockSpec((1, 1, HG_DK, HG_DK), lambda b, h, ci: (b, h, 0, 0))
    in_specs = [tok(0), tok(1), tok(2), tok(3), per_h, per_h,
                pl.BlockSpec((C, C), lambda b, h, ci: (0, 0))]
    args = [hg, hg, hg, hg, lb, gn, ltri]
    if has_s0:
        in_specs.append(st)
        args.append(s0)
    kernel = functools.partial(_hgrn_kernel, C=C, c=HG_SUB, n_valid=n_valid, has_s0=has_s0)
    return pl.pallas_call(
        kernel,
        grid=(B, HG_HEADS, nc),
        in_specs=in_specs,
        out_specs=[pl.BlockSpec((C, HG_DK), lambda b, h, ci: (b * nc + ci, h)), st],
        out_shape=[jax.ShapeDtypeStruct((B * L, HG_W), BF16),
                   jax.ShapeDtypeStruct((B, HG_HEADS, HG_DK, HG_DK), F32)],
        scratch_shapes=[pltpu.VMEM((HG_DK, HG_DK), F32)],
        compiler_params=_cparams(("parallel", "parallel", "arbitrary")),
        name="hgrn",
    )(*args)


def _post_kernel(x_ref, osb_ref, oh_ref, wo1_ref, wo2_ref, g1_ref, b1_ref, wmq_ref, mk_ref, mv_ref,
                 wmo_ref, g2_ref, b2_ref, wr_ref, br_ref, h2_ref, h2b_ref, idx_ref, gate_ref):
    x = x_ref[...]
    mix = _dot(osb_ref[...], wo1_ref[...]) + _dot(oh_ref[...], wo2_ref[...])
    h1 = _layer_norm(ALPHA * x + mix, g1_ref[...], b1_ref[...])
    qb = _dot(h1.astype(BF16), wmq_ref[...]).astype(BF16)
    outs = []
    for hd in range(MEM_HEADS):
        sl = slice(hd * MEM_DH, (hd + 1) * MEM_DH)
        s = _dot_nt(qb[:, sl], mk_ref[0, :, sl])
        p = jnp.exp(s - jnp.max(s, axis=-1, keepdims=True))
        inv = 1.0 / jnp.sum(p, axis=-1, keepdims=True)
        outs.append(_dot(p.astype(BF16), mv_ref[0, :, sl]) * inv)
    att = jnp.concatenate(outs, axis=-1).astype(BF16)
    h2 = _layer_norm(ALPHA * h1 + _dot(att, wmo_ref[...]), g2_ref[...], b2_ref[...])
    h2_ref[...] = h2
    h2b = h2.astype(BF16)
    h2b_ref[...] = h2b

    lane = lax.broadcasted_iota(jnp.int32, (x.shape[0], LANES), 1)
    logits = jnp.where(lane < N_EXPERTS, _dot(h2b, wr_ref[...]) + br_ref[...], -jnp.inf)
    vals, idxs = [], []
    for _ in range(TOP_K):
        m = jnp.max(logits, axis=-1, keepdims=True)
        idx = jnp.min(jnp.where(logits == m, lane, LANES), axis=-1, keepdims=True)
        logits = jnp.where(lane == idx, -jnp.inf, logits)
        vals.append(m)
        idxs.append(idx)
    es = [jnp.exp(vk - vals[0]) for vk in vals]
    inv = 1.0 / (es[0] + es[1] + es[2] + es[3])
    idx_out = jnp.zeros_like(lane)
    gate_out = jnp.zeros(lane.shape, F32)
    for k in range(TOP_K):
        idx_out = jnp.where(lane == k, idxs[k], idx_out)
        gate_out = jnp.where(lane == k, es[k] * inv, gate_out)
    idx_ref[...] = idx_out
    gate_ref[...] = gate_out


def _post(x2, osb, oh, mk_b, mv_b, w, B, L):
    tq = min(256, L)
    nq = L // tq
    T = B * L
    row = lambda b, i: (b * nq + i, 0)
    const = lambda b, i: (0, 0)
    tile = lambda n: pl.BlockSpec((tq, n), row)
    full = lambda a: pl.BlockSpec(a.shape, const)
    mem = pl.BlockSpec((1, mk_b.shape[1], D_MODEL), lambda b, i: (b, 0, 0))
    return pl.pallas_call(
        _post_kernel,
        grid=(B, nq),
        in_specs=[tile(D_MODEL), tile(SB_W), tile(HG_W), full(w["wo1"]), full(w["wo2"]),
                  full(w["ln1_g"]), full(w["ln1_b"]), full(w["w_mq"]), mem, mem, full(w["w_mo"]),
                  full(w["ln2_g"]), full(w["ln2_b"]), full(w["w_router"]), full(w["b_router"])],
        out_specs=[tile(D_MODEL), tile(D_MODEL), tile(LANES), tile(LANES)],
        out_shape=[jax.ShapeDtypeStruct((T, D_MODEL), F32), jax.ShapeDtypeStruct((T, D_MODEL), BF16),
                   jax.ShapeDtypeStruct((T, LANES), jnp.int32), jax.ShapeDtypeStruct((T, LANES), F32)],
        compiler_params=_cparams(("parallel", "parallel")),
        name="post",
    )(x2, osb, oh, w["wo1"], w["wo2"], w["ln1_g"], w["ln1_b"], w["w_mq"], mk_b, mv_b, w["w_mo"],
      w["ln2_g"], w["ln2_b"], w["w_router"], w["b_router"])


def _expert_kernel(be_ref, x_ref, wgu_ref, bgu_ref, wd_ref, bd_ref, o_ref):
    hgu = _dot(x_ref[...], wgu_ref[0]) + bgu_ref[0]
    d_ff = hgu.shape[1] // 2
    gate = jnp.minimum(hgu[:, :d_ff], SWIGLU_LIMIT)
    lin = jnp.clip(hgu[:, d_ff:], -SWIGLU_LIMIT, SWIGLU_LIMIT)
    act = gate * jax.nn.sigmoid(SWIGLU_ALPHA * gate) * (lin + 1.0)
    o_ref[...] = (_dot(act.astype(BF16), wd_ref[0]) + bd_ref[0]).astype(BF16)


def _experts(xg, block_e, wgu_b, bgu, wd_b, bd):
    n_rows = xg.shape[0]
    bm = MOE_BM
    d_ff2 = wgu_b.shape[2]
    ex = lambda i, be: (be[i], 0, 0)
    return pl.pallas_call(
        _expert_kernel,
        grid_spec=pltpu.PrefetchScalarGridSpec(
            num_scalar_prefetch=1,
            grid=(n_rows // bm,),
            in_specs=[pl.BlockSpec((bm, D_MODEL), lambda i, be: (i, 0)),
                      pl.BlockSpec((1, D_MODEL, d_ff2), ex),
                      pl.BlockSpec((1, 1, d_ff2), ex),
                      pl.BlockSpec((1, d_ff2 // 2, D_MODEL), ex),
                      pl.BlockSpec((1, 1, D_MODEL), ex)],
            out_specs=pl.BlockSpec((bm, D_MODEL), lambda i, be: (i, 0))),
        out_shape=jax.ShapeDtypeStruct((n_rows, D_MODEL), BF16),
        compiler_params=_cparams(("arbitrary",)),
        name="experts",
    )(block_e, xg, wgu_b, bgu, wd_b, bd)


def _combine_kernel(h_ref, ff_ref, gate_ref, g_ref, b_ref, y_ref):
    gates = gate_ref[...]
    ff = gates[:, 0:1] * ff_ref[0].astype(F32)
    for k in range(1, TOP_K):
        ff = ff + gates[:, k:k + 1] * ff_ref[k].astype(F32)
    y_ref[...] = _layer_norm(ALPHA * h_ref[...] + ff, g_ref[...], b_ref[...])


def _combine(h2, ffg, gates, g3, b3, row_off):
    T = h2.shape[0]
    tq = min(256, T)
    off = row_off // tq
    row = lambda i: (i, 0)
    const = lambda i: (0, 0)
    return pl.pallas_call(
        _combine_kernel,
        grid=(T // tq,),
        in_specs=[pl.BlockSpec((tq, D_MODEL), row),
                  pl.BlockSpec((TOP_K, tq, D_MODEL), lambda i: (0, off + i, 0)),
                  pl.BlockSpec((tq, LANES), row),
                  pl.BlockSpec((1, D_MODEL), const), pl.BlockSpec((1, D_MODEL), const)],
        out_specs=pl.BlockSpec((tq, D_MODEL), row),
        out_shape=jax.ShapeDtypeStruct((T, D_MODEL), F32),
        compiler_params=_cparams(("parallel",)),
        name="combine",
    )(h2, ffg, gates, g3, b3)


def _routing(top_idx, bm):
    T, K = top_idx.shape
    A = T * K
    e_flat = top_idx.reshape(A)
    onehot = (e_flat[:, None] == jnp.arange(N_EXPERTS, dtype=jnp.int32)[None, :]).astype(jnp.int32)
    csum = jnp.cumsum(onehot, axis=0)
    rank = jnp.take_along_axis(csum, e_flat[:, None], axis=1)[:, 0] - 1
    counts = csum[-1]
    padded = (counts + bm - 1) // bm * bm
    pad_end = jnp.cumsum(padded)
    pad_start = pad_end - padded
    slot = (pad_start[e_flat] + rank).astype(jnp.int32)
    n_blocks = -(-A // bm) + N_EXPERTS
    tok_of_slot = jnp.zeros((n_blocks * bm,), jnp.int32).at[slot].set(
        jnp.arange(A, dtype=jnp.int32) // K)
    block_e = jnp.minimum(jnp.searchsorted(pad_end, jnp.arange(n_blocks, dtype=jnp.int32) * bm,
                                           side="right"), N_EXPERTS - 1).astype(jnp.int32)
    return slot.reshape(T, K), tok_of_slot, block_e


def _pad_rows(a, n):
    B, l, W = a.shape
    return jnp.pad(a, ((0, 0), (0, n - l), (0, 0))).reshape(B * n, W)


def kernel(x_prompt, x_sample, mem_prompt, cache_sb_k, cache_sb_v, state_hgrn, cache_mem_k, cache_mem_v, page_table, w_in, sb_bias, hg_lb_logits, hg_norm_g, w_out, ln1_g, ln1_b, w_mq, w_mk, w_mv, w_mo, ln2_g, ln2_b, w_router, b_router, w_gate_up, b_gate_up, w_down, b_down, ln3_g, ln3_b):
    B, L, D = x_prompt.shape
    DB, LS, _ = x_sample.shape
    M = mem_prompt.shape[1]
    Tp, Ts = B * L, DB * LS

    col_scale = jnp.concatenate([jnp.full((SB_W,), SB_DH ** -0.5, F32),
                                 jnp.ones((w_in.shape[2] - SB_W,), F32)])
    w_in_b = (w_in[0] * col_scale).astype(BF16)
    w_out_b = w_out[0].astype(BF16)
    row1 = lambda a: a.reshape(1, -1).astype(F32)
    w = {
        "wo1": w_out_b[:SB_W], "wo2": w_out_b[SB_W:],
        "ln1_g": row1(ln1_g[0]), "ln1_b": row1(ln1_b[0]),
        "w_mq": (w_mq[0] * (MEM_DH ** -0.5)).astype(BF16), "w_mo": w_mo[0].astype(BF16),
        "ln2_g": row1(ln2_g[0]), "ln2_b": row1(ln2_b[0]),
        "w_router": jnp.pad(w_router[0], ((0, 0), (0, LANES - N_EXPERTS))).astype(BF16),
        "b_router": jnp.pad(row1(b_router[0]), ((0, 0), (0, LANES - N_EXPERTS))),
    }
    lb = jnp.cumsum(jax.nn.softmax(hg_lb_logits.astype(F32), axis=0), axis=0)[0].reshape(1, HG_W)
    gn = row1(hg_norm_g[0])
    bias = sb_bias[0].astype(F32)

    xp2 = x_prompt.reshape(Tp, D)
    q_p, k_p, v_p, kb_p, vb_p, hg_p = _inproj(xp2, w_in_b)
    osb_p = _sb_prompt(q_p, kb_p, vb_p, bias, B, L)
    oh_p, s_p = _hgrn(hg_p, lb, gn, None, B, L, HG_CHUNK)
    mk_p, mv_p, mkb_p, mvb_p = _memkv(mem_prompt.reshape(B * M, D), w_mk[0].astype(BF16),
                                      w_mv[0].astype(BF16))
    h2_p, h2b_p, idx_p, gate_p = _post(xp2, osb_p, oh_p, mkb_p.reshape(B, M, D),
                                       mvb_p.reshape(B, M, D), w, B, L)

    xs2 = x_sample.reshape(Ts, D)
    q_s, k_s, v_s, kb_s, vb_s, hg_s = _inproj(xs2, w_in_b)
    n_pool, page = cache_sb_k.shape[1], cache_sb_k.shape[2]
    osb_s = _sb_sample(q_s, kb_s, vb_s, cache_sb_k[0].reshape(n_pool, page, SB_W),
                       cache_sb_v[0].reshape(n_pool, page, SB_W), page_table, bias, DB, LS)
    hg_s_pad = _pad_rows(hg_s.reshape(DB, LS, 4 * HG_W), HG_CHUNK)
    oh_s, s_s = _hgrn(hg_s_pad, lb, gn, state_hgrn[0], DB, HG_CHUNK, LS)
    oh_s = oh_s.reshape(DB, HG_CHUNK, HG_W)[:, :LS]
    h2_s, h2b_s, idx_s, gate_s = _post(
        _pad_rows(x_sample, ROW_PAD), _pad_rows(osb_s.astype(BF16), ROW_PAD), _pad_rows(oh_s, ROW_PAD),
        cache_mem_k[0].reshape(DB, M, D).astype(BF16), cache_mem_v[0].reshape(DB, M, D).astype(BF16),
        w, DB, ROW_PAD)
    unpad = lambda a: a.reshape(DB, ROW_PAD, a.shape[-1])[:, :LS].reshape(Ts, a.shape[-1])
    h2_s, h2b_s, idx_s, gate_s = unpad(h2_s), unpad(h2b_s), unpad(idx_s), unpad(gate_s)

    top_idx = jnp.concatenate([idx_p[:, :TOP_K], idx_s[:, :TOP_K]], axis=0)
    slot, tok_of_slot, block_e = _routing(top_idx, MOE_BM)
    h2b_all = jnp.concatenate([h2b_p, h2b_s], axis=0)
    xg = jnp.take(h2b_all, tok_of_slot, axis=0)
    eo = _experts(xg, block_e, w_gate_up[0].astype(BF16), b_gate_up[0][:, None, :],
                  w_down[0].astype(BF16), b_down[0][:, None, :])
    ffg = jnp.take(eo, slot.T.reshape(-1), axis=0).reshape(TOP_K, Tp + Ts, D)
    g3, b3 = row1(ln3_g[0]), row1(ln3_b[0])
    y_p = _combine(h2_p, ffg, gate_p, g3, b3, 0)
    y_s = _combine(h2_s, ffg, gate_s, g3, b3, Tp)

    sbshape = lambda a, b_, l_: a.reshape(1, b_, l_, SB_HEADS, SB_DH)
    memshape = lambda a: a.reshape(1, B, M, MEM_HEADS, MEM_DH)
    return (y_p.reshape(B, L, D), y_s.reshape(DB, LS, D),
            sbshape(k_p, B, L), sbshape(v_p, B, L), s_p[None],
            memshape(mk_p), memshape(mv_p),
            sbshape(k_s, DB, LS), sbshape(v_s, DB, LS), s_s[None])
```

```python
import functools
import math

import numpy as np
import jax
import jax.numpy as jnp
from jax import lax
from jax.experimental import pallas as pl
from jax.experimental.pallas import tpu as pltpu

F32 = jnp.float32
BF16 = jnp.bfloat16

D_MODEL = 1024
SB_HEADS = 8
SB_DH = 64
SB_W = SB_HEADS * SB_DH
HG_HEADS = 4
HG_DK = 128
HG_W = HG_HEADS * HG_DK
MEM_HEADS = 4
MEM_DH = D_MODEL // MEM_HEADS
N_EXPERTS = 32
TOP_K = 4
SWIGLU_ALPHA = 1.702
SWIGLU_LIMIT = 7.0
DEPTH = 1
ALPHA = (2 * DEPTH) ** 0.25
LN_EPS = 1e-5
RMS_EPS = 1e-6

LANES = 128
VMEM_LIMIT = 52 * 1024 * 1024
SB_TQ = 256
SB_TK = 128
HG_CHUNK = 128
MOE_BM = 512
POST_TQ = 512
ROW_PAD = 16
SAMPLE_ROWS = 16
LOG2E = math.log2(math.e)


def _cparams(sem):
    return pltpu.CompilerParams(dimension_semantics=sem, vmem_limit_bytes=VMEM_LIMIT)


def _dot(a, b, **kw):
    return jnp.dot(a, b, preferred_element_type=F32, **kw)


def _dot_nt(a, b):
    return lax.dot_general(a, b, (((1,), (1,)), ((), ())), preferred_element_type=F32)


def _layer_norm(x, g, b):
    mu = jnp.mean(x, axis=-1, keepdims=True)
    xc = x - mu
    var = jnp.mean(xc * xc, axis=-1, keepdims=True)
    return xc * lax.rsqrt(var + LN_EPS) * g + b


def _inproj_kernel(x_ref, w_ref, q_ref, k_ref, v_ref, kb_ref, vb_ref, hg_ref, *, transposed):
    xb = x_ref[...].astype(BF16)
    q = _dot(xb, w_ref[:, 0:SB_W])
    k = _dot(xb, w_ref[:, SB_W:2 * SB_W])
    kb_ref[...] = k.astype(BF16)
    v = _dot(xb, w_ref[:, 2 * SB_W:3 * SB_W])
    if transposed:
        q_ref[...] = q.T.astype(BF16)
        k_ref[0] = k.T
        vt = v.T
        v_ref[0] = vt
        vb_ref[...] = vt.astype(BF16)
    else:
        q_ref[...] = q.astype(BF16)
        k_ref[...] = k
        v_ref[...] = v
        vb_ref[...] = v.astype(BF16)
    hg_ref[...] = _dot(xb, w_ref[:, 3 * SB_W:])


def _inproj(x2, w_in_b, seq_len, transposed):
    T = x2.shape[0]
    tm = min(256, T)
    n_in = w_in_b.shape[1]
    per_seq = seq_len // tm
    row = lambda i: (i, 0)
    col = lambda i: (0, i)
    if transposed:
        bf_spec, bf_shape = pl.BlockSpec((SB_W, tm), col), (SB_W, T)
        f_spec = pl.BlockSpec((1, SB_W, tm), lambda i: (i // per_seq, 0, i % per_seq))
        f_shape = (T // seq_len, SB_W, seq_len)
    else:
        bf_spec, bf_shape = pl.BlockSpec((tm, SB_W), row), (T, SB_W)
        f_spec, f_shape = bf_spec, bf_shape
    return pl.pallas_call(
        functools.partial(_inproj_kernel, transposed=transposed),
        grid=(T // tm,),
        in_specs=[pl.BlockSpec((tm, D_MODEL), row),
                  pl.BlockSpec((D_MODEL, n_in), lambda i: (0, 0))],
        out_specs=[bf_spec, f_spec, f_spec, pl.BlockSpec((tm, SB_W), row),
                   bf_spec, pl.BlockSpec((tm, 4 * HG_W), row)],
        out_shape=[jax.ShapeDtypeStruct(bf_shape, BF16), jax.ShapeDtypeStruct(f_shape, F32),
                   jax.ShapeDtypeStruct(f_shape, F32), jax.ShapeDtypeStruct((T, SB_W), BF16),
                   jax.ShapeDtypeStruct(bf_shape, BF16), jax.ShapeDtypeStruct((T, 4 * HG_W), F32)],
        compiler_params=_cparams(("parallel",)),
        name="inproj",
    )(x2, w_in_b)


def _memkv_kernel(m_ref, wk_ref, wv_ref, k_ref, v_ref, kb_ref, vb_ref):
    mb = m_ref[...].astype(BF16)
    k = _dot(mb, wk_ref[...])
    v = _dot(mb, wv_ref[...])
    k_ref[...] = k
    v_ref[...] = v
    kb_ref[...] = k.astype(BF16)
    vb_ref[...] = v.astype(BF16)


def _memkv(mem2, wk_b, wv_b):
    T = mem2.shape[0]
    tm = min(256, T)
    row = lambda i: (i, 0)
    full = lambda i: (0, 0)
    sds = lambda dt: jax.ShapeDtypeStruct((T, D_MODEL), dt)
    return pl.pallas_call(
        _memkv_kernel,
        grid=(T // tm,),
        in_specs=[pl.BlockSpec((tm, D_MODEL), row), pl.BlockSpec((D_MODEL, D_MODEL), full),
                  pl.BlockSpec((D_MODEL, D_MODEL), full)],
        out_specs=[pl.BlockSpec((tm, D_MODEL), row)] * 4,
        out_shape=[sds(F32), sds(F32), sds(BF16), sds(BF16)],
        compiler_params=_cparams(("parallel",)),
        name="memkv",
    )(mem2, wk_b, wv_b)


def _sb_weights(z, ue, carry, mask):
    sp = jnp.maximum(z, 0.0) + jnp.log(1.0 + jnp.exp(-jnp.abs(z)))
    l1m = -sp
    if mask is not None:
        l1m = jnp.where(mask, l1m, 0.0)
    cs = _dot(l1m.astype(BF16), ue)
    tk = z.shape[1]
    a = jnp.exp((z - sp) + cs[:, :tk] + carry)
    if mask is not None:
        a = jnp.where(mask, a, 0.0)
    return a.astype(BF16), carry + cs[:, tk:]


def _sb_ue(tk):
    j = np.arange(tk)[:, None]
    s = np.arange(tk)[None, :]
    u = (j > s).astype(np.float32)
    return jnp.asarray(np.concatenate([u, np.ones((tk, tk), np.float32)], axis=1), dtype=BF16)


def _sb_softplus_t(zt, mask):
    neg_abs = lax.bitcast_convert_type(
        lax.bitcast_convert_type(zt, jnp.uint32) | jnp.uint32(0x80000000), F32)
    sp = jnp.maximum(zt, 0.0) + jnp.log(1.0 + jnp.exp2(neg_abs)) * LOG2E
    lbeta = zt - sp
    if mask is not None:
        sp = jnp.where(mask, sp, 0.0)
    return sp.astype(BF16), lbeta


def _sb_weights_t(lbeta, spb, cs, carry_row, mask):
    a = jnp.exp2(lbeta - cs - carry_row)
    if mask is not None:
        a = jnp.where(mask, a, 0.0)
    return a.astype(BF16), carry_row + cs[0:1, :] + spb[0:1, :].astype(F32)


def _sb_prompt_kernel(bias_ref, qt_ref, k_ref, vt_ref, ut_ref, o_ref, qm_ref, carry_ref, acc_ref,
                      *, tq, tk):
    i = pl.program_id(1)
    rpb = tq // tk
    pair_w = 2 * SB_DH
    ut = ut_ref[...]
    lo_rows = lax.broadcasted_iota(jnp.int32, (pair_w, tq), 0) < SB_DH
    for p in range(SB_HEADS // 2):
        qp = qt_ref[p * pair_w:(p + 1) * pair_w, :]
        qm_ref[2 * p] = jnp.where(lo_rows, qp, jnp.zeros_like(qp))
        qm_ref[2 * p + 1] = jnp.where(lo_rows, jnp.zeros_like(qp), qp)
    carry_ref[...] = jnp.zeros_like(carry_ref)
    acc_ref[...] = jnp.zeros_like(acc_ref)
    d_rc = (lax.broadcasted_iota(jnp.int32, (tk, tq), 0)
            - lax.broadcasted_iota(jnp.int32, (tk, tq), 1))

    def step(j, masked):
        ks = pl.multiple_of(j * tk, tk)
        mask = (d_rc < (i * tq - j * tk)) if masked else None
        heads = range(SB_HEADS)
        zts = [_dot(k_ref[pl.ds(ks, tk), (h // 2) * pair_w:(h // 2 + 1) * pair_w], qm_ref[h])
               + bias_ref[h] for h in heads]
        sps = [_sb_softplus_t(zts[h], mask) for h in heads]
        css = [_dot(ut, sps[h][0]) for h in heads]
        pvs = []
        for h in heads:
            a, c_new = _sb_weights_t(sps[h][1], sps[h][0], css[h], carry_ref[h:h + 1, :], mask)
            carry_ref[h:h + 1, :] = c_new
            pvs.append(_dot(vt_ref[(h // 2) * pair_w:(h // 2 + 1) * pair_w, pl.ds(ks, tk)], a))
        for p in range(SB_HEADS // 2):
            acc_ref[p] += jnp.where(lo_rows, pvs[2 * p], pvs[2 * p + 1])

    for jj in range(rpb):
        step((i + 1) * rpb - 1 - jj, True)

    def body(n, c):
        step(i * rpb - 1 - n, False)
        return c

    lax.fori_loop(0, i * rpb, body, 0)
    for p in range(SB_HEADS // 2):
        o_ref[:, p * pair_w:(p + 1) * pair_w] = acc_ref[p].T.astype(BF16)


def _sb_prompt(qt, kb, vt, bias2, B, L):
    tq = min(SB_TQ, L)
    tk = SB_TK
    nq = L // tq
    ut = jnp.asarray(np.triu(np.ones((tk, tk), np.float32), 1), dtype=BF16)
    kernel = functools.partial(_sb_prompt_kernel, tq=tq, tk=tk)
    return pl.pallas_call(
        kernel,
        grid_spec=pltpu.PrefetchScalarGridSpec(
            num_scalar_prefetch=1,
            grid=(B, nq),
            in_specs=[pl.BlockSpec((SB_W, tq), lambda b, i, s: (0, b * nq + i)),
                      pl.BlockSpec((L, SB_W), lambda b, i, s: (b, 0)),
                      pl.BlockSpec((SB_W, L), lambda b, i, s: (0, b)),
                      pl.BlockSpec((tk, tk), lambda b, i, s: (0, 0))],
            out_specs=pl.BlockSpec((tq, SB_W), lambda b, i, s: (b * nq + i, 0)),
            scratch_shapes=[pltpu.VMEM((SB_HEADS, 2 * SB_DH, tq), BF16),
                            pltpu.VMEM((SB_HEADS, tq), F32),
                            pltpu.VMEM((SB_HEADS // 2, 2 * SB_DH, tq), F32)]),
        out_shape=jax.ShapeDtypeStruct((B * L, SB_W), BF16),
        compiler_params=_cparams(("parallel", "arbitrary")),
        name="sb_prompt",
    )(bias2, qt, kb, vt, ut)


def _sb_sample_kernel(pt_ref, q_ref, bias_ref, kn_ref, vn_ref, ue_ref, *rest, npg):
    k_refs = rest[:npg]
    v_refs = rest[npg:2 * npg]
    o_ref = rest[2 * npg]
    carry_ref, acc_ref = rest[2 * npg + 1:]
    j = pl.program_id(1)
    R = SAMPLE_ROWS
    rows = R * SB_HEADS
    bias = bias_ref[...]
    ue = ue_ref[...]
    qh = [q_ref[0, h * R:(h + 1) * R, :].astype(BF16) for h in range(SB_HEADS)]

    def block(kt_of, vt_of, mask):
        z = jnp.concatenate([_dot(qh[h], kt_of(h)) for h in range(SB_HEADS)], axis=0) + bias
        a, c_new = _sb_weights(z, ue, carry_ref[...], mask)
        carry_ref[...] = c_new
        acc_ref[...] += jnp.concatenate(
            [_dot_nt(a[h * R:(h + 1) * R], vt_of(h)) for h in range(SB_HEADS)], axis=1)

    @pl.when(j == 0)
    def _():
        carry_ref[...] = jnp.zeros_like(carry_ref)
        acc_ref[...] = jnp.zeros_like(acc_ref)
        col = lax.broadcasted_iota(jnp.int32, (rows, SB_TK), 1)
        t = lax.broadcasted_iota(jnp.int32, (rows, SB_TK), 0) % R
        block(lambda h: kn_ref[0, h], lambda h: vn_ref[0, h], col < t)

    for p in range(npg):
        head = lambda ref: (lambda h: ref[0, h].astype(BF16))
        block(head(k_refs[p]), head(v_refs[p]), None)

    @pl.when(j == pl.num_programs(1) - 1)
    def _():
        o_ref[0] = acc_ref[...]


def _sb_sample(q_s, kb_s, vb_s, cache_k, cache_v, page_table, sb_bias, DB, lq):
    n_pool, page = cache_k.shape[0], cache_k.shape[1]
    n_pages = page_table.shape[1]
    npg = math.gcd(n_pages, 8)
    R = SAMPLE_ROWS
    rows = R * SB_HEADS
    by_head = lambda a: a.reshape(DB, lq, SB_HEADS, SB_DH)
    q8 = jnp.pad(by_head(q_s).transpose(0, 2, 1, 3).astype(F32),
                 ((0, 0), (0, 0), (0, R - lq), (0, 0))).reshape(DB, rows, SB_DH)
    pad_new = lambda a: jnp.pad(by_head(a).transpose(0, 2, 3, 1), ((0, 0), (0, 0), (0, 0), (0, page - lq)))
    bias_rows = jnp.broadcast_to(jnp.repeat(sb_bias.astype(F32), R)[:, None], (rows, SB_TK))
    flat = lambda c: c.transpose(0, 2, 3, 1)

    def page_spec(p):
        return pl.BlockSpec((1, SB_HEADS, SB_DH, page),
                            lambda b, j, pt: (pt[b, n_pages - 1 - (j * npg + p)], 0, 0, 0))

    new_spec = pl.BlockSpec((1, SB_HEADS, SB_DH, page), lambda b, j, pt: (b, 0, 0, 0))
    const = lambda b, j, pt: (0, 0)
    kernel = functools.partial(_sb_sample_kernel, npg=npg)
    out = pl.pallas_call(
        kernel,
        grid_spec=pltpu.PrefetchScalarGridSpec(
            num_scalar_prefetch=1,
            grid=(DB, n_pages // npg),
            in_specs=[pl.BlockSpec((1, rows, SB_DH), lambda b, j, pt: (b, 0, 0)),
                      pl.BlockSpec((rows, SB_TK), const),
                      new_spec, new_spec,
                      pl.BlockSpec((SB_TK, 2 * SB_TK), const)]
                     + [page_spec(p) for p in range(npg)] * 2,
            out_specs=pl.BlockSpec((1, R, SB_W), lambda b, j, pt: (b, 0, 0)),
            scratch_shapes=[pltpu.VMEM((rows, SB_TK), F32), pltpu.VMEM((R, SB_W), F32)]),
        out_shape=jax.ShapeDtypeStruct((DB, R, SB_W), F32),
        compiler_params=_cparams(("parallel", "arbitrary")),
        name="sb_sample",
    )(page_table, q8, bias_rows, pad_new(kb_s), pad_new(vb_s), _sb_ue(SB_TK),
      *([flat(cache_k)] * npg), *([flat(cache_v)] * npg))
    return out[:, :lq]


def _hgrn_tables(C):
    t = np.arange(C)[:, None]
    j = np.arange(C)[None, :]
    sums = [(j <= t)]
    upper, pair = [], []
    w = C // 2
    while w >= 1:
        m = (t // (2 * w)) * (2 * w) + w
        up = (t % (2 * w)) >= w
        sums.append(np.where(up, (j >= m) & (j <= t), (j > t) & (j <= m - 1)))
        upper.append(np.broadcast_to(up, (C, HG_DK)))
        pair.append((t // (2 * w) == j // (2 * w)) & up & ((j % (2 * w)) < w))
        w //= 2
    pair.append(t == j)
    f = lambda a: np.concatenate([x.astype(np.float32) for x in a], axis=0)
    sums2 = jnp.asarray(np.tile(f(sums), (1, 2)), dtype=BF16)
    return (sums2, jnp.asarray(f(upper).reshape(len(upper), C, HG_DK)),
            jnp.asarray(f(pair).reshape(len(pair), C, C)))


def _hgrn_kernel(*refs, C, n_valid, has_s0):
    if has_s0:
        (q_ref, f_ref, i_ref, og_ref, lb_ref, gn_ref, sums_ref, upper_ref, pair_ref, s0_ref,
         o_ref, sout_ref, S_ref) = refs
    else:
        (q_ref, f_ref, i_ref, og_ref, lb_ref, gn_ref, sums_ref, upper_ref, pair_ref,
         o_ref, sout_ref, S_ref) = refs
    ci = pl.program_id(1)
    heads = range(HG_HEADS)
    hs = lambda h: slice(h * HG_DK, (h + 1) * HG_DK)

    @pl.when(ci == 0)
    def _():
        S_ref[...] = s0_ref[0] if has_s0 else jnp.zeros_like(S_ref)

    n_lvl = upper_ref.shape[0]
    lb = lb_ref[...]
    qh = q_ref[...]
    qh = qh * jax.nn.sigmoid(qh)
    f = lb + (1.0 - lb) * jax.nn.sigmoid(f_ref[...])
    g = jnp.log(f)
    kk = 1.0 - f
    if n_valid < C:
        valid = lax.broadcasted_iota(jnp.int32, (C, HG_W), 0) < n_valid
        g = jnp.where(valid, g, 0.0)
        kk = jnp.where(valid, kk, 0.0)
    vb = i_ref[...].astype(BF16)
    g_hi = g.astype(BF16)
    g_lo = (g - g_hi.astype(F32)).astype(BF16)
    g2 = jnp.concatenate([g_hi, g_lo], axis=0)
    sums = sums_ref[...]
    e_all = [_dot(sums, g2[:, hs(h)]) for h in heads]
    att = [_dot_nt(qh[:, hs(h)].astype(BF16), kk[:, hs(h)].astype(BF16)) * pair_ref[n_lvl]
           for h in heads]
    for lvl in range(n_lvl):
        up = upper_ref[lvl] > 0.5
        for h in heads:
            x = jnp.exp(e_all[h][(lvl + 1) * C:(lvl + 2) * C])
            y = (x * jnp.where(up, qh[:, hs(h)], kk[:, hs(h)])).astype(BF16)
            att[h] = att[h] + _dot_nt(y, y) * pair_ref[lvl]
    outs = []
    for h in heads:
        b = e_all[h][0:C]
        S = S_ref[h]
        o = (_dot((qh[:, hs(h)] * jnp.exp(b)).astype(BF16), S.astype(BF16))
             + _dot(att[h].astype(BF16), vb[:, hs(h)]))
        b_end = b[C - 1:C]
        kd = kk[:, hs(h)] * jnp.exp(b_end - b)
        decay_col = jnp.transpose(jnp.broadcast_to(jnp.exp(b_end), (HG_DK, HG_DK)))
        s_new = decay_col * S + _dot(jnp.transpose(kd).astype(BF16), vb[:, hs(h)])
        S_ref[h] = s_new
        sout_ref[0, h] = s_new
        outs.append(o * lax.rsqrt(jnp.mean(o * o, axis=-1, keepdims=True) + RMS_EPS))
    o_all = jnp.concatenate(outs, axis=1)
    o_ref[...] = (o_all * gn_ref[...] * jax.nn.sigmoid(og_ref[...])).astype(BF16)


def _hgrn(hg, lb, gn, s0, B, L, n_valid):
    C = HG_CHUNK
    nc = L // C
    has_s0 = s0 is not None
    sums, upper, pair = _hgrn_tables(C)
    tok = lambda grp: pl.BlockSpec((C, HG_W), lambda b, ci: (b * nc + ci, grp))
    per_h = pl.BlockSpec((1, HG_W), lambda b, ci: (0, 0))
    st = pl.BlockSpec((1, HG_HEADS, HG_DK, HG_DK), lambda b, ci: (b, 0, 0, 0))
    table = lambda a: pl.BlockSpec(a.shape, lambda b, ci: (0,) * a.ndim)
    in_specs = [tok(0), tok(1), tok(2), tok(3), per_h, per_h, table(sums), table(upper), table(pair)]
    args = [hg, hg, hg, hg, lb, gn, sums, upper, pair]
    if has_s0:
        in_specs.append(st)
        args.append(s0)
    kernel = functools.partial(_hgrn_kernel, C=C, n_valid=n_valid, has_s0=has_s0)
    return pl.pallas_call(
        kernel,
        grid=(B, nc),
        in_specs=in_specs,
        out_specs=[pl.BlockSpec((C, HG_W), lambda b, ci: (b * nc + ci, 0)), st],
        out_shape=[jax.ShapeDtypeStruct((B * L, HG_W), BF16),
                   jax.ShapeDtypeStruct((B, HG_HEADS, HG_DK, HG_DK), F32)],
        scratch_shapes=[pltpu.VMEM((HG_HEADS, HG_DK, HG_DK), F32)],
        compiler_params=_cparams(("parallel", "arbitrary")),
        name="hgrn",
    )(*args)


def _post_kernel(x_ref, osb_ref, oh_ref, wo1_ref, wo2_ref, g1_ref, b1_ref, wmq_ref, mk_ref, mv_ref,
                 wmo_ref, g2_ref, b2_ref, wr_ref, br_ref, base_ref, before_ref,
                 h2_ref, h2b_ref, idx_ref, gate_ref, rank_ref, cnt_ref, run_ref, *, n_valid):
    first = jnp.logical_and(pl.program_id(0) == 0, pl.program_id(1) == 0)

    @pl.when(first)
    def _():
        run_ref[...] = base_ref[...]

    x = x_ref[...]
    mix = _dot(osb_ref[...], wo1_ref[...]) + _dot(oh_ref[...], wo2_ref[...])
    h1 = _layer_norm(ALPHA * x + mix, g1_ref[...], b1_ref[...])
    qb = _dot(h1.astype(BF16), wmq_ref[...]).astype(BF16)
    outs = []
    for hd in range(MEM_HEADS):
        sl = slice(hd * MEM_DH, (hd + 1) * MEM_DH)
        s = _dot_nt(qb[:, sl], mk_ref[0, :, sl])
        p = jnp.exp(s - jnp.max(s, axis=-1, keepdims=True))
        inv = 1.0 / jnp.sum(p, axis=-1, keepdims=True)
        outs.append(_dot(p.astype(BF16), mv_ref[0, :, sl]) * inv)
    att = jnp.concatenate(outs, axis=-1).astype(BF16)
    h2 = _layer_norm(ALPHA * h1 + _dot(att, wmo_ref[...]), g2_ref[...], b2_ref[...])
    h2_ref[...] = h2
    h2b = h2.astype(BF16)
    h2b_ref[...] = h2b

    lane = lax.broadcasted_iota(jnp.int32, (x.shape[0], LANES), 1)
    logits = jnp.where(lane < N_EXPERTS, _dot(h2b, wr_ref[...]) + br_ref[...], -jnp.inf)
    vals, idxs = [], []
    for _ in range(TOP_K):
        m = jnp.max(logits, axis=-1, keepdims=True)
        idx = jnp.min(jnp.where(logits == m, lane, LANES), axis=-1, keepdims=True)
        logits = jnp.where(lane == idx, -jnp.inf, logits)
        vals.append(m)
        idxs.append(idx)
    es = [jnp.exp(vk - vals[0]) for vk in vals]
    inv = 1.0 / (es[0] + es[1] + es[2] + es[3])
    hit = [lane == idxs[k] for k in range(TOP_K)]
    chosen = (hit[0] | hit[1]) | (hit[2] | hit[3])
    if n_valid is not None:
        chosen = chosen & (lax.broadcasted_iota(jnp.int32, chosen.shape, 0) < n_valid)
    onehot = jnp.where(chosen, 1.0, 0.0)
    earlier = _dot(before_ref[...], onehot.astype(BF16)) + run_ref[...]
    idx_out = jnp.zeros_like(lane)
    rank_out = jnp.zeros_like(lane)
    gate_out = jnp.zeros(lane.shape, F32)
    for k in range(TOP_K):
        rank = jnp.sum(jnp.where(hit[k], earlier, 0.0), axis=-1, keepdims=True).astype(jnp.int32)
        idx_out = jnp.where(lane == k, idxs[k], idx_out)
        rank_out = jnp.where(lane == k, rank, rank_out)
        gate_out = jnp.where(lane == k, es[k] * inv, gate_out)
    idx_ref[...] = idx_out
    rank_ref[...] = rank_out
    gate_ref[...] = gate_out
    last = x.shape[0] - 1
    run_ref[...] = earlier[last:last + 1, :] + onehot[last:last + 1, :]
    cnt_ref[...] = run_ref[...]


def _post(x2, osb, oh, mk_b, mv_b, w, base_counts, B, L, n_valid):
    tq = min(POST_TQ, L)
    nq = L // tq
    T = B * L
    row = lambda b, i: (b * nq + i, 0)
    const = lambda b, i: (0, 0)
    tile = lambda n: pl.BlockSpec((tq, n), row)
    full = lambda a: pl.BlockSpec(a.shape, const)
    mem = pl.BlockSpec((1, mk_b.shape[1], D_MODEL), lambda b, i: (b, 0, 0))
    before = jnp.asarray(np.tril(np.ones((tq, tq), np.float32), -1), dtype=BF16)
    return pl.pallas_call(
        functools.partial(_post_kernel, n_valid=n_valid),
        grid=(B, nq),
        in_specs=[tile(D_MODEL), tile(SB_W), tile(HG_W), full(w["wo1"]), full(w["wo2"]),
                  full(w["ln1_g"]), full(w["ln1_b"]), full(w["w_mq"]), mem, mem, full(w["w_mo"]),
                  full(w["ln2_g"]), full(w["ln2_b"]), full(w["w_router"]), full(w["b_router"]),
                  full(base_counts), full(before)],
        out_specs=[tile(D_MODEL), tile(D_MODEL), tile(LANES), tile(LANES), tile(LANES),
                   pl.BlockSpec((1, LANES), const)],
        out_shape=[jax.ShapeDtypeStruct((T, D_MODEL), F32), jax.ShapeDtypeStruct((T, D_MODEL), BF16),
                   jax.ShapeDtypeStruct((T, LANES), jnp.int32), jax.ShapeDtypeStruct((T, LANES), F32),
                   jax.ShapeDtypeStruct((T, LANES), jnp.int32), jax.ShapeDtypeStruct((1, LANES), F32)],
        scratch_shapes=[pltpu.VMEM((1, LANES), F32)],
        compiler_params=_cparams(("arbitrary", "arbitrary")),
        name="post",
    )(x2, osb, oh, w["wo1"], w["wo2"], w["ln1_g"], w["ln1_b"], w["w_mq"], mk_b, mv_b, w["w_mo"],
      w["ln2_g"], w["ln2_b"], w["w_router"], w["b_router"], base_counts, before)


def _expert_kernel(be_ref, x_ref, wgu_ref, bgu_ref, wd_ref, bd_ref, o_ref):
    hgu = _dot(x_ref[...], wgu_ref[0]) + bgu_ref[0]
    d_ff = hgu.shape[1] // 2
    gate = jnp.minimum(hgu[:, :d_ff], SWIGLU_LIMIT)
    lin = jnp.clip(hgu[:, d_ff:], -SWIGLU_LIMIT, SWIGLU_LIMIT)
    act = gate * jax.nn.sigmoid(SWIGLU_ALPHA * gate) * (lin + 1.0)
    o_ref[...] = (_dot(act.astype(BF16), wd_ref[0]) + bd_ref[0]).astype(BF16)


def _experts(xg, block_e, wgu_b, bgu, wd_b, bd):
    n_rows = xg.shape[0]
    bm = MOE_BM
    d_ff2 = wgu_b.shape[2]
    ex = lambda i, be: (be[i], 0, 0)
    return pl.pallas_call(
        _expert_kernel,
        grid_spec=pltpu.PrefetchScalarGridSpec(
            num_scalar_prefetch=1,
            grid=(n_rows // bm,),
            in_specs=[pl.BlockSpec((bm, D_MODEL), lambda i, be: (i, 0)),
                      pl.BlockSpec((1, D_MODEL, d_ff2), ex),
                      pl.BlockSpec((1, 1, d_ff2), ex),
                      pl.BlockSpec((1, d_ff2 // 2, D_MODEL), ex),
                      pl.BlockSpec((1, 1, D_MODEL), ex)],
            out_specs=pl.BlockSpec((bm, D_MODEL), lambda i, be: (i, 0))),
        out_shape=jax.ShapeDtypeStruct((n_rows, D_MODEL), BF16),
        compiler_params=_cparams(("arbitrary",)),
        name="experts",
    )(block_e, xg, wgu_b, bgu, wd_b, bd)


def _combine_kernel(h_ref, ff_ref, gate_ref, g_ref, b_ref, y_ref):
    gates = gate_ref[...]
    ff = gates[:, 0:1] * ff_ref[0].astype(F32)
    for k in range(1, TOP_K):
        ff = ff + gates[:, k:k + 1] * ff_ref[k].astype(F32)
    y_ref[...] = _layer_norm(ALPHA * h_ref[...] + ff, g_ref[...], b_ref[...])


def _combine(h2, ffg, gates, g3, b3, row_off):
    T = h2.shape[0]
    tq = min(256, T)
    off = row_off // tq
    row = lambda i: (i, 0)
    const = lambda i: (0, 0)
    return pl.pallas_call(
        _combine_kernel,
        grid=(T // tq,),
        in_specs=[pl.BlockSpec((tq, D_MODEL), row),
                  pl.BlockSpec((TOP_K, tq, D_MODEL), lambda i: (0, off + i, 0)),
                  pl.BlockSpec((tq, LANES), row),
                  pl.BlockSpec((1, D_MODEL), const), pl.BlockSpec((1, D_MODEL), const)],
        out_specs=pl.BlockSpec((tq, D_MODEL), row),
        out_shape=jax.ShapeDtypeStruct((T, D_MODEL), F32),
        compiler_params=_cparams(("parallel",)),
        name="combine",
    )(h2, ffg, gates, g3, b3)


def _routing(top_idx, rank, counts, bm):
    T, K = top_idx.shape
    A = T * K
    e_flat = top_idx.reshape(A)
    rank = rank.reshape(A)
    padded = (counts + bm - 1) // bm * bm
    pad_end = jnp.cumsum(padded)
    pad_start = pad_end - padded
    slot = (pad_start[e_flat] + rank).astype(jnp.int32)
    n_blocks = -(-A // bm) + N_EXPERTS
    tok_of_slot = jnp.zeros((n_blocks * bm,), jnp.int32).at[slot].set(
        jnp.arange(A, dtype=jnp.int32) // K, unique_indices=True, mode="promise_in_bounds")
    block_e = jnp.minimum(jnp.searchsorted(pad_end, jnp.arange(n_blocks, dtype=jnp.int32) * bm,
                                           side="right"), N_EXPERTS - 1).astype(jnp.int32)
    return slot.reshape(T, K), tok_of_slot, block_e


def _pad_rows(a, n):
    B, l, W = a.shape
    return jnp.pad(a, ((0, 0), (0, n - l), (0, 0))).reshape(B * n, W)


def kernel(x_prompt, x_sample, mem_prompt, cache_sb_k, cache_sb_v, state_hgrn, cache_mem_k, cache_mem_v, page_table, w_in, sb_bias, hg_lb_logits, hg_norm_g, w_out, ln1_g, ln1_b, w_mq, w_mk, w_mv, w_mo, ln2_g, ln2_b, w_router, b_router, w_gate_up, b_gate_up, w_down, b_down, ln3_g, ln3_b):
    B, L, D = x_prompt.shape
    DB, LS, _ = x_sample.shape
    M = mem_prompt.shape[1]
    Tp, Ts = B * L, DB * LS

    q_scale = lambda s: jnp.concatenate([jnp.full((SB_W,), s, F32),
                                         jnp.ones((w_in.shape[2] - SB_W,), F32)])
    w_in_s = (w_in[0] * q_scale(SB_DH ** -0.5)).astype(BF16)
    w_in_p = (w_in[0] * q_scale(SB_DH ** -0.5 * LOG2E)).astype(BF16)
    w_out_b = w_out[0].astype(BF16)
    row1 = lambda a: a.reshape(1, -1).astype(F32)
    w = {
        "wo1": w_out_b[:SB_W], "wo2": w_out_b[SB_W:],
        "ln1_g": row1(ln1_g[0]), "ln1_b": row1(ln1_b[0]),
        "w_mq": (w_mq[0] * (MEM_DH ** -0.5)).astype(BF16), "w_mo": w_mo[0].astype(BF16),
        "ln2_g": row1(ln2_g[0]), "ln2_b": row1(ln2_b[0]),
        "w_router": jnp.pad(w_router[0], ((0, 0), (0, LANES - N_EXPERTS))).astype(BF16),
        "b_router": jnp.pad(row1(b_router[0]), ((0, 0), (0, LANES - N_EXPERTS))),
    }
    lb = jnp.cumsum(jax.nn.softmax(hg_lb_logits.astype(F32), axis=0), axis=0)[0].reshape(1, HG_W)
    gn = row1(hg_norm_g[0])
    bias = sb_bias[0].astype(F32)

    xp2 = x_prompt.reshape(Tp, D)
    qt_p, kt_p, vt32_p, kb_p, vt_p, hg_p = _inproj(xp2, w_in_p, L, True)
    osb_p = _sb_prompt(qt_p, kb_p, vt_p, bias * LOG2E, B, L)
    oh_p, s_p = _hgrn(hg_p, lb, gn, None, B, L, HG_CHUNK)
    mk_p, mv_p, mkb_p, mvb_p = _memkv(mem_prompt.reshape(B * M, D), w_mk[0].astype(BF16),
                                      w_mv[0].astype(BF16))
    h2_p, h2b_p, idx_p, gate_p, rank_p, cnt_p = _post(
        xp2, osb_p, oh_p, mkb_p.reshape(B, M, D), mvb_p.reshape(B, M, D), w,
        jnp.zeros((1, LANES), F32), B, L, None)

    xs2 = x_sample.reshape(Ts, D)
    q_s, k_s, v_s, kb_s, vb_s, hg_s = _inproj(xs2, w_in_s, LS, False)
    osb_s = _sb_sample(q_s, kb_s, vb_s, cache_sb_k[0], cache_sb_v[0], page_table, bias, DB, LS)
    hg_s_pad = _pad_rows(hg_s.reshape(DB, LS, 4 * HG_W), HG_CHUNK)
    oh_s, s_s = _hgrn(hg_s_pad, lb, gn, state_hgrn[0], DB, HG_CHUNK, LS)
    oh_s = oh_s.reshape(DB, HG_CHUNK, HG_W)[:, :LS]
    h2_s, h2b_s, idx_s, gate_s, rank_s, cnt_all = _post(
        _pad_rows(x_sample, ROW_PAD), _pad_rows(osb_s.astype(BF16), ROW_PAD), _pad_rows(oh_s, ROW_PAD),
        cache_mem_k[0].reshape(DB, M, D).astype(BF16), cache_mem_v[0].reshape(DB, M, D).astype(BF16),
        w, cnt_p, DB, ROW_PAD, LS)
    unpad = lambda a: a.reshape(DB, ROW_PAD, a.shape[-1])[:, :LS].reshape(Ts, a.shape[-1])
    h2_s, h2b_s, idx_s, gate_s, rank_s = (unpad(h2_s), unpad(h2b_s), unpad(idx_s), unpad(gate_s),
                                          unpad(rank_s))

    both = lambda a_p, a_s: jnp.concatenate([a_p[:, :TOP_K], a_s[:, :TOP_K]], axis=0)
    slot, tok_of_slot, block_e = _routing(both(idx_p, idx_s), both(rank_p, rank_s),
                                          cnt_all[0, :N_EXPERTS].astype(jnp.int32), MOE_BM)
    h2b_all = jnp.concatenate([h2b_p, h2b_s], axis=0)
    xg = h2b_all.at[tok_of_slot].get(mode="promise_in_bounds")
    eo = _experts(xg, block_e, w_gate_up[0].astype(BF16), b_gate_up[0][:, None, :],
                  w_down[0].astype(BF16), b_down[0][:, None, :])
    ffg = eo.at[slot.T.reshape(-1)].get(mode="promise_in_bounds").reshape(TOP_K, Tp + Ts, D)
    g3, b3 = row1(ln3_g[0]), row1(ln3_b[0])
    y_p = _combine(h2_p, ffg, gate_p, g3, b3, 0)
    y_s = _combine(h2_s, ffg, gate_s, g3, b3, Tp)

    sbshape = lambda a, b_, l_: a.reshape(1, b_, l_, SB_HEADS, SB_DH)
    sbshape_t = lambda a: a.reshape(1, B, SB_HEADS, SB_DH, L).transpose(0, 1, 4, 2, 3)
    memshape = lambda a: a.reshape(1, B, M, MEM_HEADS, MEM_DH)
    return (y_p.reshape(B, L, D), y_s.reshape(DB, LS, D),
            sbshape_t(kt_p), sbshape_t(vt32_p), s_p[None],
            memshape(mk_p), memshape(mv_p),
            sbshape(k_s, DB, LS), sbshape(v_s, DB, LS), s_s[None])
```

```python
import functools
import math

import numpy as np
import jax
import jax.numpy as jnp
from jax import lax
from jax.experimental import pallas as pl
from jax.experimental.pallas import tpu as pltpu

F32 = jnp.float32
BF16 = jnp.bfloat16

D_MODEL = 1024
SB_HEADS = 8
SB_DH = 64
SB_W = SB_HEADS * SB_DH
HG_HEADS = 4
HG_DK = 128
HG_W = HG_HEADS * HG_DK
MEM_HEADS = 4
MEM_DH = D_MODEL // MEM_HEADS
N_EXPERTS = 32
TOP_K = 4
SWIGLU_ALPHA = 1.702
SWIGLU_LIMIT = 7.0
DEPTH = 1
ALPHA = (2 * DEPTH) ** 0.25
LN_EPS = 1e-5
RMS_EPS = 1e-6

LANES = 128
VMEM_LIMIT = 52 * 1024 * 1024
SB_TQ = 256
SB_TK = 128
HG_CHUNK = 128
MOE_BM = 512
MOE_BM_SAMPLE = 128
POST_TQ = 512
ROW_PAD = 16
SAMPLE_ROWS = 16
LOG2E = math.log2(math.e)


def _cparams(sem):
    return pltpu.CompilerParams(dimension_semantics=sem, vmem_limit_bytes=VMEM_LIMIT)


def _dot(a, b, **kw):
    return jnp.dot(a, b, preferred_element_type=F32, **kw)


def _dot_nt(a, b):
    return lax.dot_general(a, b, (((1,), (1,)), ((), ())), preferred_element_type=F32)


def _layer_norm(x, g, b):
    mu = jnp.mean(x, axis=-1, keepdims=True)
    xc = x - mu
    var = jnp.mean(xc * xc, axis=-1, keepdims=True)
    return xc * lax.rsqrt(var + LN_EPS) * g + b


def _inproj_kernel(x_ref, w_ref, q_ref, k_ref, v_ref, kb_ref, vb_ref, hg_ref, *, transposed):
    xb = x_ref[...].astype(BF16)
    q = _dot(xb, w_ref[:, 0:SB_W])
    k = _dot(xb, w_ref[:, SB_W:2 * SB_W])
    kb_ref[...] = k.astype(BF16)
    v = _dot(xb, w_ref[:, 2 * SB_W:3 * SB_W])
    if transposed:
        q_ref[...] = q.T.astype(BF16)
        k_ref[0] = k.T
        vt = v.T
        v_ref[0] = vt
        vb_ref[...] = vt.astype(BF16)
    else:
        q_ref[...] = q.astype(BF16)
        k_ref[...] = k
        v_ref[...] = v
        vb_ref[...] = v.astype(BF16)
    hg_ref[...] = _dot(xb, w_ref[:, 3 * SB_W:])


def _inproj(x2, w_in_b, seq_len, transposed):
    T = x2.shape[0]
    tm = math.gcd(256, T)
    n_in = w_in_b.shape[1]
    per_seq = seq_len // tm
    row = lambda i: (i, 0)
    col = lambda i: (0, i)
    if transposed:
        bf_spec, bf_shape = pl.BlockSpec((SB_W, tm), col), (SB_W, T)
        f_spec = pl.BlockSpec((1, SB_W, tm), lambda i: (i // per_seq, 0, i % per_seq))
        f_shape = (T // seq_len, SB_W, seq_len)
    else:
        bf_spec, bf_shape = pl.BlockSpec((tm, SB_W), row), (T, SB_W)
        f_spec, f_shape = bf_spec, bf_shape
    return pl.pallas_call(
        functools.partial(_inproj_kernel, transposed=transposed),
        grid=(T // tm,),
        in_specs=[pl.BlockSpec((tm, D_MODEL), row),
                  pl.BlockSpec((D_MODEL, n_in), lambda i: (0, 0))],
        out_specs=[bf_spec, f_spec, f_spec, pl.BlockSpec((tm, SB_W), row),
                   bf_spec, pl.BlockSpec((tm, 4 * HG_W), row)],
        out_shape=[jax.ShapeDtypeStruct(bf_shape, BF16), jax.ShapeDtypeStruct(f_shape, F32),
                   jax.ShapeDtypeStruct(f_shape, F32), jax.ShapeDtypeStruct((T, SB_W), BF16),
                   jax.ShapeDtypeStruct(bf_shape, BF16), jax.ShapeDtypeStruct((T, 4 * HG_W), F32)],
        compiler_params=_cparams(("parallel",)),
        name="inproj",
    )(x2, w_in_b)


def _memkv_kernel(m_ref, wk_ref, wv_ref, k_ref, v_ref, kb_ref, vb_ref):
    mb = m_ref[...].astype(BF16)
    k = _dot(mb, wk_ref[...])
    v = _dot(mb, wv_ref[...])
    k_ref[...] = k
    v_ref[...] = v
    kb_ref[...] = k.astype(BF16)
    vb_ref[...] = v.astype(BF16)


def _memkv(mem2, wk_b, wv_b):
    T = mem2.shape[0]
    tm = math.gcd(256, T)
    row = lambda i: (i, 0)
    full = lambda i: (0, 0)
    sds = lambda dt: jax.ShapeDtypeStruct((T, D_MODEL), dt)
    return pl.pallas_call(
        _memkv_kernel,
        grid=(T // tm,),
        in_specs=[pl.BlockSpec((tm, D_MODEL), row), pl.BlockSpec((D_MODEL, D_MODEL), full),
                  pl.BlockSpec((D_MODEL, D_MODEL), full)],
        out_specs=[pl.BlockSpec((tm, D_MODEL), row)] * 4,
        out_shape=[sds(F32), sds(F32), sds(BF16), sds(BF16)],
        compiler_params=_cparams(("parallel",)),
        name="memkv",
    )(mem2, wk_b, wv_b)


def _sb_weights(z, ue, carry, mask):
    sp = jnp.maximum(z, 0.0) + jnp.log(1.0 + jnp.exp(-jnp.abs(z)))
    l1m = -sp
    if mask is not None:
        l1m = jnp.where(mask, l1m, 0.0)
    cs = _dot(l1m.astype(BF16), ue)
    tk = z.shape[1]
    a = jnp.exp((z - sp) + cs[:, :tk] + carry)
    if mask is not None:
        a = jnp.where(mask, a, 0.0)
    return a.astype(BF16), carry + cs[:, tk:]


def _sb_ue(tk):
    j = np.arange(tk)[:, None]
    s = np.arange(tk)[None, :]
    u = (j > s).astype(np.float32)
    return jnp.asarray(np.concatenate([u, np.ones((tk, tk), np.float32)], axis=1), dtype=BF16)


def _sb_softplus_t(zt, mask):
    neg_abs = lax.bitcast_convert_type(
        lax.bitcast_convert_type(zt, jnp.uint32) | jnp.uint32(0x80000000), F32)
    sp = jnp.maximum(zt, 0.0) + jnp.log(1.0 + jnp.exp2(neg_abs)) * LOG2E
    lbeta = zt - sp
    if mask is not None:
        sp = jnp.where(mask, sp, 0.0)
    return sp.astype(BF16), lbeta


def _sb_weights_t(lbeta, spb, cs, carry_row, mask):
    a = jnp.exp2(lbeta - cs - carry_row)
    if mask is not None:
        a = jnp.where(mask, a, 0.0)
    return a.astype(BF16), carry_row + cs[0:1, :] + spb[0:1, :].astype(F32)


def _sb_prompt_kernel(bias_ref, qt_ref, k_ref, vt_ref, ut_ref, o_ref, qm_ref, carry_ref, acc_ref,
                      *, tq, tk):
    i = pl.program_id(1)
    rpb = tq // tk
    pair_w = 2 * SB_DH
    ut = ut_ref[...]
    lo_rows = lax.broadcasted_iota(jnp.int32, (pair_w, tq), 0) < SB_DH
    for p in range(SB_HEADS // 2):
        qp = qt_ref[p * pair_w:(p + 1) * pair_w, :]
        qm_ref[2 * p] = jnp.where(lo_rows, qp, jnp.zeros_like(qp))
        qm_ref[2 * p + 1] = jnp.where(lo_rows, jnp.zeros_like(qp), qp)
    carry_ref[...] = jnp.zeros_like(carry_ref)
    acc_ref[...] = jnp.zeros_like(acc_ref)
    d_rc = (lax.broadcasted_iota(jnp.int32, (tk, tq), 0)
            - lax.broadcasted_iota(jnp.int32, (tk, tq), 1))

    def step(j, masked):
        ks = pl.multiple_of(j * tk, tk)
        mask = (d_rc < (i * tq - j * tk)) if masked else None
        heads = range(SB_HEADS)
        zts = [_dot(k_ref[pl.ds(ks, tk), (h // 2) * pair_w:(h // 2 + 1) * pair_w], qm_ref[h])
               + bias_ref[h] for h in heads]
        sps = [_sb_softplus_t(zts[h], mask) for h in heads]
        css = [_dot(ut, sps[h][0]) for h in heads]
        pvs = []
        for h in heads:
            a, c_new = _sb_weights_t(sps[h][1], sps[h][0], css[h], carry_ref[h:h + 1, :], mask)
            carry_ref[h:h + 1, :] = c_new
            pvs.append(_dot(vt_ref[(h // 2) * pair_w:(h // 2 + 1) * pair_w, pl.ds(ks, tk)], a))
        for p in range(SB_HEADS // 2):
            acc_ref[p] += jnp.where(lo_rows, pvs[2 * p], pvs[2 * p + 1])

    for jj in range(rpb):
        step((i + 1) * rpb - 1 - jj, True)

    def body(n, c):
        step(i * rpb - 1 - n, False)
        return c

    lax.fori_loop(0, i * rpb, body, 0)
    for p in range(SB_HEADS // 2):
        o_ref[:, p * pair_w:(p + 1) * pair_w] = acc_ref[p].T.astype(BF16)


def _sb_prompt(qt, kb, vt, bias2, B, L):
    tq = math.gcd(SB_TQ, L)
    tk = SB_TK
    nq = L // tq
    ut = jnp.asarray(np.triu(np.ones((tk, tk), np.float32), 1), dtype=BF16)
    kernel = functools.partial(_sb_prompt_kernel, tq=tq, tk=tk)
    return pl.pallas_call(
        kernel,
        grid_spec=pltpu.PrefetchScalarGridSpec(
            num_scalar_prefetch=1,
            grid=(B, nq),
            in_specs=[pl.BlockSpec((SB_W, tq), lambda b, i, s: (0, b * nq + i)),
                      pl.BlockSpec((L, SB_W), lambda b, i, s: (b, 0)),
                      pl.BlockSpec((SB_W, L), lambda b, i, s: (0, b)),
                      pl.BlockSpec((tk, tk), lambda b, i, s: (0, 0))],
            out_specs=pl.BlockSpec((tq, SB_W), lambda b, i, s: (b * nq + i, 0)),
            scratch_shapes=[pltpu.VMEM((SB_HEADS, 2 * SB_DH, tq), BF16),
                            pltpu.VMEM((SB_HEADS, tq), F32),
                            pltpu.VMEM((SB_HEADS // 2, 2 * SB_DH, tq), F32)]),
        out_shape=jax.ShapeDtypeStruct((B * L, SB_W), BF16),
        compiler_params=_cparams(("parallel", "arbitrary")),
        name="sb_prompt",
    )(bias2, qt, kb, vt, ut)


def _sb_sample_kernel(pt_ref, q_ref, bias_ref, kn_ref, vn_ref, ue_ref, *rest, npg):
    k_refs = rest[:npg]
    v_refs = rest[npg:2 * npg]
    o_ref = rest[2 * npg]
    carry_ref, acc_ref = rest[2 * npg + 1:]
    j = pl.program_id(1)
    R = SAMPLE_ROWS
    rows = R * SB_HEADS
    bias = bias_ref[...]
    ue = ue_ref[...]
    qh = [q_ref[0, h * R:(h + 1) * R, :].astype(BF16) for h in range(SB_HEADS)]

    def block(kt_of, vt_of, mask):
        z = jnp.concatenate([_dot(qh[h], kt_of(h)) for h in range(SB_HEADS)], axis=0) + bias
        a, c_new = _sb_weights(z, ue, carry_ref[...], mask)
        carry_ref[...] = c_new
        acc_ref[...] += jnp.concatenate(
            [_dot_nt(a[h * R:(h + 1) * R], vt_of(h)) for h in range(SB_HEADS)], axis=1)

    @pl.when(j == 0)
    def _():
        carry_ref[...] = jnp.zeros_like(carry_ref)
        acc_ref[...] = jnp.zeros_like(acc_ref)
        col = lax.broadcasted_iota(jnp.int32, (rows, SB_TK), 1)
        t = lax.broadcasted_iota(jnp.int32, (rows, SB_TK), 0) % R
        block(lambda h: kn_ref[0, h], lambda h: vn_ref[0, h], col < t)

    tk = SB_TK
    order = range(npg - 1, -1, -1)
    cat = lambda refs, h: jnp.concatenate([refs[p][0, h].astype(BF16) for p in order], axis=1)
    z = jnp.concatenate([_dot(qh[h], cat(k_refs, h)) for h in range(SB_HEADS)], axis=0) + bias[:, 0:1]
    sp = jnp.maximum(z, 0.0) + jnp.log(1.0 + jnp.exp(-jnp.abs(z)))
    l1m = (-sp).astype(BF16)
    lbeta = z - sp
    cs = [_dot(l1m[:, jb * tk:(jb + 1) * tk], ue) for jb in range(npg)]
    carry = carry_ref[...]
    a_blocks = [None] * npg
    for jb in order:
        a_blocks[jb] = jnp.exp(lbeta[:, jb * tk:(jb + 1) * tk] + cs[jb][:, :tk] + carry).astype(BF16)
        carry = carry + cs[jb][:, tk:]
    carry_ref[...] = carry
    a = jnp.concatenate(a_blocks, axis=1)
    acc_ref[...] += jnp.concatenate(
        [_dot_nt(a[h * R:(h + 1) * R], cat(v_refs, h)) for h in range(SB_HEADS)], axis=1)

    @pl.when(j == pl.num_programs(1) - 1)
    def _():
        o_ref[0] = acc_ref[...]


def _sb_sample(q_s, kb_s, vb_s, cache_k, cache_v, page_table, sb_bias, DB, lq):
    n_pool, page = cache_k.shape[0], cache_k.shape[1]
    n_pages = page_table.shape[1]
    npg = math.gcd(n_pages, 8)
    R = SAMPLE_ROWS
    rows = R * SB_HEADS
    by_head = lambda a: a.reshape(DB, lq, SB_HEADS, SB_DH)
    q8 = jnp.pad(by_head(q_s).transpose(0, 2, 1, 3).astype(F32),
                 ((0, 0), (0, 0), (0, R - lq), (0, 0))).reshape(DB, rows, SB_DH)
    pad_new = lambda a: jnp.pad(by_head(a).transpose(0, 2, 3, 1), ((0, 0), (0, 0), (0, 0), (0, page - lq)))
    bias_rows = jnp.broadcast_to(jnp.repeat(sb_bias.astype(F32), R)[:, None], (rows, SB_TK))
    flat = lambda c: c.transpose(0, 2, 3, 1)

    def page_spec(p):
        return pl.BlockSpec((1, SB_HEADS, SB_DH, page),
                            lambda b, j, pt: (pt[b, n_pages - 1 - (j * npg + p)], 0, 0, 0))

    new_spec = pl.BlockSpec((1, SB_HEADS, SB_DH, page), lambda b, j, pt: (b, 0, 0, 0))
    const = lambda b, j, pt: (0, 0)
    kernel = functools.partial(_sb_sample_kernel, npg=npg)
    out = pl.pallas_call(
        kernel,
        grid_spec=pltpu.PrefetchScalarGridSpec(
            num_scalar_prefetch=1,
            grid=(DB, n_pages // npg),
            in_specs=[pl.BlockSpec((1, rows, SB_DH), lambda b, j, pt: (b, 0, 0)),
                      pl.BlockSpec((rows, SB_TK), const),
                      new_spec, new_spec,
                      pl.BlockSpec((SB_TK, 2 * SB_TK), const)]
                     + [page_spec(p) for p in range(npg)] * 2,
            out_specs=pl.BlockSpec((1, R, SB_W), lambda b, j, pt: (b, 0, 0)),
            scratch_shapes=[pltpu.VMEM((rows, SB_TK), F32), pltpu.VMEM((R, SB_W), F32)]),
        out_shape=jax.ShapeDtypeStruct((DB, R, SB_W), F32),
        compiler_params=_cparams(("parallel", "arbitrary")),
        name="sb_sample",
    )(page_table, q8, bias_rows, pad_new(kb_s), pad_new(vb_s), _sb_ue(SB_TK),
      *([flat(cache_k)] * npg), *([flat(cache_v)] * npg))
    return out[:, :lq]


def _hgrn_tables(C):
    t = np.arange(C)[:, None]
    j = np.arange(C)[None, :]
    sums = [(j <= t)]
    upper, pair = [], []
    w = C // 2
    while w >= 1:
        m = (t // (2 * w)) * (2 * w) + w
        up = (t % (2 * w)) >= w
        sums.append(np.where(up, (j >= m) & (j <= t), (j > t) & (j <= m - 1)))
        upper.append(np.broadcast_to(up, (C, HG_DK)))
        pair.append((t // (2 * w) == j // (2 * w)) & up & ((j % (2 * w)) < w))
        w //= 2
    pair.append(t == j)
    f = lambda a: np.concatenate([x.astype(np.float32) for x in a], axis=0)
    sums2 = jnp.asarray(np.tile(f(sums), (1, 2)), dtype=BF16)
    return (sums2, jnp.asarray(f(upper).reshape(len(upper), C, HG_DK)),
            jnp.asarray(f(pair).reshape(len(pair), C, C)))


def _hgrn_kernel(*refs, C, n_valid, has_s0):
    if has_s0:
        (q_ref, f_ref, i_ref, og_ref, lb_ref, gn_ref, sums_ref, upper_ref, pair_ref, s0_ref,
         o_ref, sout_ref, S_ref) = refs
    else:
        (q_ref, f_ref, i_ref, og_ref, lb_ref, gn_ref, sums_ref, upper_ref, pair_ref,
         o_ref, sout_ref, S_ref) = refs
    ci = pl.program_id(1)
    heads = range(HG_HEADS)
    hs = lambda h: slice(h * HG_DK, (h + 1) * HG_DK)

    @pl.when(ci == 0)
    def _():
        S_ref[...] = s0_ref[0] if has_s0 else jnp.zeros_like(S_ref)

    n_lvl = upper_ref.shape[0]
    lb = lb_ref[...]
    qh = q_ref[...]
    qh = qh * jax.nn.sigmoid(qh)
    f = lb + (1.0 - lb) * jax.nn.sigmoid(f_ref[...])
    g = jnp.log(f)
    kk = 1.0 - f
    if n_valid < C:
        valid = lax.broadcasted_iota(jnp.int32, (C, HG_W), 0) < n_valid
        g = jnp.where(valid, g, 0.0)
        kk = jnp.where(valid, kk, 0.0)
    vb = i_ref[...].astype(BF16)
    g_hi = g.astype(BF16)
    g_lo = (g - g_hi.astype(F32)).astype(BF16)
    g2 = jnp.concatenate([g_hi, g_lo], axis=0)
    sums = sums_ref[...]
    e_all = [_dot(sums, g2[:, hs(h)]) for h in heads]
    att = [_dot_nt(qh[:, hs(h)].astype(BF16), kk[:, hs(h)].astype(BF16)) * pair_ref[n_lvl]
           for h in heads]
    for lvl in range(n_lvl):
        up = upper_ref[lvl] > 0.5
        for h in heads:
            x = jnp.exp(e_all[h][(lvl + 1) * C:(lvl + 2) * C])
            y = (x * jnp.where(up, qh[:, hs(h)], kk[:, hs(h)])).astype(BF16)
            att[h] = att[h] + _dot_nt(y, y) * pair_ref[lvl]
    outs = []
    for h in heads:
        b = e_all[h][0:C]
        S = S_ref[h]
        o = (_dot((qh[:, hs(h)] * jnp.exp(b)).astype(BF16), S.astype(BF16))
             + _dot(att[h].astype(BF16), vb[:, hs(h)]))
        b_end = b[C - 1:C]
        kd = kk[:, hs(h)] * jnp.exp(b_end - b)
        decay_col = jnp.transpose(jnp.broadcast_to(jnp.exp(b_end), (HG_DK, HG_DK)))
        s_new = decay_col * S + _dot(jnp.transpose(kd).astype(BF16), vb[:, hs(h)])
        S_ref[h] = s_new
        sout_ref[0, h] = s_new
        outs.append(o * lax.rsqrt(jnp.mean(o * o, axis=-1, keepdims=True) + RMS_EPS))
    o_all = jnp.concatenate(outs, axis=1)
    o_ref[...] = (o_all * gn_ref[...] * jax.nn.sigmoid(og_ref[...])).astype(BF16)


def _hgrn(hg, lb, gn, s0, B, L, n_valid):
    C = HG_CHUNK
    nc = L // C
    has_s0 = s0 is not None
    sums, upper, pair = _hgrn_tables(C)
    tok = lambda grp: pl.BlockSpec((C, HG_W), lambda b, ci: (b * nc + ci, grp))
    per_h = pl.BlockSpec((1, HG_W), lambda b, ci: (0, 0))
    st = pl.BlockSpec((1, HG_HEADS, HG_DK, HG_DK), lambda b, ci: (b, 0, 0, 0))
    table = lambda a: pl.BlockSpec(a.shape, lambda b, ci: (0,) * a.ndim)
    in_specs = [tok(0), tok(1), tok(2), tok(3), per_h, per_h, table(sums), table(upper), table(pair)]
    args = [hg, hg, hg, hg, lb, gn, sums, upper, pair]
    if has_s0:
        in_specs.append(st)
        args.append(s0)
    kernel = functools.partial(_hgrn_kernel, C=C, n_valid=n_valid, has_s0=has_s0)
    return pl.pallas_call(
        kernel,
        grid=(B, nc),
        in_specs=in_specs,
        out_specs=[pl.BlockSpec((C, HG_W), lambda b, ci: (b * nc + ci, 0)), st],
        out_shape=[jax.ShapeDtypeStruct((B * L, HG_W), BF16),
                   jax.ShapeDtypeStruct((B, HG_HEADS, HG_DK, HG_DK), F32)],
        scratch_shapes=[pltpu.VMEM((HG_HEADS, HG_DK, HG_DK), F32)],
        compiler_params=_cparams(("parallel", "arbitrary")),
        name="hgrn",
    )(*args)


def _post_kernel(x_ref, osb_ref, oh_ref, wo1_ref, wo2_ref, g1_ref, b1_ref, wmq_ref, mk_ref, mv_ref,
                 wmo_ref, g2_ref, b2_ref, wr_ref, br_ref, base_ref, before_ref,
                 h2_ref, h2b_ref, idx_ref, gate_ref, rank_ref, cnt_ref, run_ref, *, n_valid):
    first = jnp.logical_and(pl.program_id(0) == 0, pl.program_id(1) == 0)

    @pl.when(first)
    def _():
        run_ref[...] = base_ref[...]

    x = x_ref[...]
    mix = _dot(osb_ref[...], wo1_ref[...]) + _dot(oh_ref[...], wo2_ref[...])
    h1 = _layer_norm(ALPHA * x + mix, g1_ref[...], b1_ref[...])
    qb = _dot(h1.astype(BF16), wmq_ref[...]).astype(BF16)
    outs = []
    for hd in range(MEM_HEADS):
        sl = slice(hd * MEM_DH, (hd + 1) * MEM_DH)
        s = _dot_nt(qb[:, sl], mk_ref[0, :, sl])
        p = jnp.exp(s - jnp.max(s, axis=-1, keepdims=True))
        inv = 1.0 / jnp.sum(p, axis=-1, keepdims=True)
        outs.append(_dot(p.astype(BF16), mv_ref[0, :, sl]) * inv)
    att = jnp.concatenate(outs, axis=-1).astype(BF16)
    h2 = _layer_norm(ALPHA * h1 + _dot(att, wmo_ref[...]), g2_ref[...], b2_ref[...])
    h2_ref[...] = h2
    h2b = h2.astype(BF16)
    h2b_ref[...] = h2b

    lane = lax.broadcasted_iota(jnp.int32, (x.shape[0], LANES), 1)
    logits = jnp.where(lane < N_EXPERTS, _dot(h2b, wr_ref[...]) + br_ref[...], -jnp.inf)
    vals, idxs = [], []
    for _ in range(TOP_K):
        m = jnp.max(logits, axis=-1, keepdims=True)
        idx = jnp.min(jnp.where(logits == m, lane, LANES), axis=-1, keepdims=True)
        logits = jnp.where(lane == idx, -jnp.inf, logits)
        vals.append(m)
        idxs.append(idx)
    es = [jnp.exp(vk - vals[0]) for vk in vals]
    inv = 1.0 / (es[0] + es[1] + es[2] + es[3])
    hit = [lane == idxs[k] for k in range(TOP_K)]
    chosen = (hit[0] | hit[1]) | (hit[2] | hit[3])
    if n_valid is not None:
        chosen = chosen & (lax.broadcasted_iota(jnp.int32, chosen.shape, 0) < n_valid)
    onehot = jnp.where(chosen, 1.0, 0.0)
    earlier = _dot(before_ref[...], onehot.astype(BF16)) + run_ref[...]
    idx_out = jnp.zeros_like(lane)
    rank_out = jnp.zeros_like(lane)
    gate_out = jnp.zeros(lane.shape, F32)
    for k in range(TOP_K):
        rank = jnp.sum(jnp.where(hit[k], earlier, 0.0), axis=-1, keepdims=True).astype(jnp.int32)
        idx_out = jnp.where(lane == k, idxs[k], idx_out)
        rank_out = jnp.where(lane == k, rank, rank_out)
        gate_out = jnp.where(lane == k, es[k] * inv, gate_out)
    idx_ref[...] = idx_out
    rank_ref[...] = rank_out
    gate_ref[...] = gate_out
    last = x.shape[0] - 1
    run_ref[...] = earlier[last:last + 1, :] + onehot[last:last + 1, :]
    cnt_ref[...] = run_ref[...]


def _post(x2, osb, oh, mk_b, mv_b, w, base_counts, B, L, n_valid):
    tq = math.gcd(POST_TQ, L)
    nq = L // tq
    T = B * L
    row = lambda b, i: (b * nq + i, 0)
    const = lambda b, i: (0, 0)
    tile = lambda n: pl.BlockSpec((tq, n), row)
    full = lambda a: pl.BlockSpec(a.shape, const)
    mem = pl.BlockSpec((1, mk_b.shape[1], D_MODEL), lambda b, i: (b, 0, 0))
    before = jnp.asarray(np.tril(np.ones((tq, tq), np.float32), -1), dtype=BF16)
    return pl.pallas_call(
        functools.partial(_post_kernel, n_valid=n_valid),
        grid=(B, nq),
        in_specs=[tile(D_MODEL), tile(SB_W), tile(HG_W), full(w["wo1"]), full(w["wo2"]),
                  full(w["ln1_g"]), full(w["ln1_b"]), full(w["w_mq"]), mem, mem, full(w["w_mo"]),
                  full(w["ln2_g"]), full(w["ln2_b"]), full(w["w_router"]), full(w["b_router"]),
                  full(base_counts), full(before)],
        out_specs=[tile(D_MODEL), tile(D_MODEL), tile(LANES), tile(LANES), tile(LANES),
                   pl.BlockSpec((1, LANES), const)],
        out_shape=[jax.ShapeDtypeStruct((T, D_MODEL), F32), jax.ShapeDtypeStruct((T, D_MODEL), BF16),
                   jax.ShapeDtypeStruct((T, LANES), jnp.int32), jax.ShapeDtypeStruct((T, LANES), F32),
                   jax.ShapeDtypeStruct((T, LANES), jnp.int32), jax.ShapeDtypeStruct((1, LANES), F32)],
        scratch_shapes=[pltpu.VMEM((1, LANES), F32)],
        compiler_params=_cparams(("arbitrary", "arbitrary")),
        name="post",
    )(x2, osb, oh, w["wo1"], w["wo2"], w["ln1_g"], w["ln1_b"], w["w_mq"], mk_b, mv_b, w["w_mo"],
      w["ln2_g"], w["ln2_b"], w["w_router"], w["b_router"], base_counts, before)


def _expert_kernel(be_ref, x_ref, wgu_ref, bgu_ref, wd_ref, bd_ref, o_ref):
    hgu = _dot(x_ref[...], wgu_ref[0]) + bgu_ref[0]
    d_ff = hgu.shape[1] // 2
    gate = jnp.minimum(hgu[:, :d_ff], SWIGLU_LIMIT)
    lin = jnp.clip(hgu[:, d_ff:], -SWIGLU_LIMIT, SWIGLU_LIMIT)
    act = gate * jax.nn.sigmoid(SWIGLU_ALPHA * gate) * (lin + 1.0)
    o_ref[...] = (_dot(act.astype(BF16), wd_ref[0]) + bd_ref[0]).astype(BF16)


def _experts(xg, block_e, wgu_b, bgu, wd_b, bd, bm):
    n_rows = xg.shape[0]
    d_ff2 = wgu_b.shape[2]
    ex = lambda i, be: (be[i], 0, 0)
    return pl.pallas_call(
        _expert_kernel,
        grid_spec=pltpu.PrefetchScalarGridSpec(
            num_scalar_prefetch=1,
            grid=(n_rows // bm,),
            in_specs=[pl.BlockSpec((bm, D_MODEL), lambda i, be: (i, 0)),
                      pl.BlockSpec((1, D_MODEL, d_ff2), ex),
                      pl.BlockSpec((1, 1, d_ff2), ex),
                      pl.BlockSpec((1, d_ff2 // 2, D_MODEL), ex),
                      pl.BlockSpec((1, 1, D_MODEL), ex)],
            out_specs=pl.BlockSpec((bm, D_MODEL), lambda i, be: (i, 0))),
        out_shape=jax.ShapeDtypeStruct((n_rows, D_MODEL), BF16),
        compiler_params=_cparams(("arbitrary",)),
        name="experts",
    )(block_e, xg, wgu_b, bgu, wd_b, bd)


def _combine_kernel(h_ref, ff_ref, gate_ref, g_ref, b_ref, y_ref):
    gates = gate_ref[...]
    ff = gates[:, 0:1] * ff_ref[0].astype(F32)
    for k in range(1, TOP_K):
        ff = ff + gates[:, k:k + 1] * ff_ref[k].astype(F32)
    y_ref[...] = _layer_norm(ALPHA * h_ref[...] + ff, g_ref[...], b_ref[...])


def _combine(h2, ffg, gates, g3, b3, row_off):
    T = h2.shape[0]
    tq = math.gcd(256, T)
    off = row_off // tq
    row = lambda i: (i, 0)
    const = lambda i: (0, 0)
    return pl.pallas_call(
        _combine_kernel,
        grid=(T // tq,),
        in_specs=[pl.BlockSpec((tq, D_MODEL), row),
                  pl.BlockSpec((TOP_K, tq, D_MODEL), lambda i: (0, off + i, 0)),
                  pl.BlockSpec((tq, LANES), row),
                  pl.BlockSpec((1, D_MODEL), const), pl.BlockSpec((1, D_MODEL), const)],
        out_specs=pl.BlockSpec((tq, D_MODEL), row),
        out_shape=jax.ShapeDtypeStruct((T, D_MODEL), F32),
        compiler_params=_cparams(("parallel",)),
        name="combine",
    )(h2, ffg, gates, g3, b3)


def _routing(top_idx, rank, counts, bm):
    T, K = top_idx.shape
    A = T * K
    e_flat = top_idx.reshape(A)
    rank = rank.reshape(A)
    padded = (counts + bm - 1) // bm * bm
    pad_end = jnp.cumsum(padded)
    pad_start = pad_end - padded
    slot = (pad_start[e_flat] + rank).astype(jnp.int32)
    n_blocks = -(-A // bm) + N_EXPERTS
    tok_of_slot = jnp.zeros((n_blocks * bm,), jnp.int32).at[slot].set(
        jnp.arange(A, dtype=jnp.int32) // K, unique_indices=True, mode="promise_in_bounds")
    block_row = jnp.arange(n_blocks, dtype=jnp.int32) * bm
    block_e = jnp.minimum(jnp.sum((pad_end[None, :] <= block_row[:, None]).astype(jnp.int32), axis=1),
                          N_EXPERTS - 1)
    return slot.reshape(T, K), tok_of_slot, block_e


def _pad_rows(a, n):
    B, l, W = a.shape
    return jnp.pad(a, ((0, 0), (0, n - l), (0, 0))).reshape(B * n, W)


def kernel(x_prompt, x_sample, mem_prompt, cache_sb_k, cache_sb_v, state_hgrn, cache_mem_k, cache_mem_v, page_table, w_in, sb_bias, hg_lb_logits, hg_norm_g, w_out, ln1_g, ln1_b, w_mq, w_mk, w_mv, w_mo, ln2_g, ln2_b, w_router, b_router, w_gate_up, b_gate_up, w_down, b_down, ln3_g, ln3_b):
    B, L, D = x_prompt.shape
    DB, LS, _ = x_sample.shape
    M = mem_prompt.shape[1]
    Tp, Ts = B * L, DB * LS

    q_scale = lambda s: jnp.concatenate([jnp.full((SB_W,), s, F32),
                                         jnp.ones((w_in.shape[2] - SB_W,), F32)])
    w_in_s = (w_in[0] * q_scale(SB_DH ** -0.5)).astype(BF16)
    w_in_p = (w_in[0] * q_scale(SB_DH ** -0.5 * LOG2E)).astype(BF16)
    w_out_b = w_out[0].astype(BF16)
    row1 = lambda a: a.reshape(1, -1).astype(F32)
    w = {
        "wo1": w_out_b[:SB_W], "wo2": w_out_b[SB_W:],
        "ln1_g": row1(ln1_g[0]), "ln1_b": row1(ln1_b[0]),
        "w_mq": (w_mq[0] * (MEM_DH ** -0.5)).astype(BF16), "w_mo": w_mo[0].astype(BF16),
        "ln2_g": row1(ln2_g[0]), "ln2_b": row1(ln2_b[0]),
        "w_router": jnp.pad(w_router[0], ((0, 0), (0, LANES - N_EXPERTS))).astype(BF16),
        "b_router": jnp.pad(row1(b_router[0]), ((0, 0), (0, LANES - N_EXPERTS))),
    }
    lb = jnp.cumsum(jax.nn.softmax(hg_lb_logits.astype(F32), axis=0), axis=0)[0].reshape(1, HG_W)
    gn = row1(hg_norm_g[0])
    bias = sb_bias[0].astype(F32)

    xp2 = x_prompt.reshape(Tp, D)
    qt_p, kt_p, vt32_p, kb_p, vt_p, hg_p = _inproj(xp2, w_in_p, L, True)
    osb_p = _sb_prompt(qt_p, kb_p, vt_p, bias * LOG2E, B, L)
    oh_p, s_p = _hgrn(hg_p, lb, gn, None, B, L, HG_CHUNK)
    mk_p, mv_p, mkb_p, mvb_p = _memkv(mem_prompt.reshape(B * M, D), w_mk[0].astype(BF16),
                                      w_mv[0].astype(BF16))
    no_counts = jnp.zeros((1, LANES), F32)
    h2_p, h2b_p, idx_p, gate_p, rank_p, cnt_p = _post(
        xp2, osb_p, oh_p, mkb_p.reshape(B, M, D), mvb_p.reshape(B, M, D), w, no_counts, B, L, None)

    xs2 = x_sample.reshape(Ts, D)
    q_s, k_s, v_s, kb_s, vb_s, hg_s = _inproj(xs2, w_in_s, LS, False)
    osb_s = _sb_sample(q_s, kb_s, vb_s, cache_sb_k[0], cache_sb_v[0], page_table, bias, DB, LS)
    hg_s_pad = _pad_rows(hg_s.reshape(DB, LS, 4 * HG_W), HG_CHUNK)
    oh_s, s_s = _hgrn(hg_s_pad, lb, gn, state_hgrn[0], DB, HG_CHUNK, LS)
    oh_s = oh_s.reshape(DB, HG_CHUNK, HG_W)[:, :LS]
    h2_s, h2b_s, idx_s, gate_s, rank_s, cnt_s = _post(
        _pad_rows(x_sample, ROW_PAD), _pad_rows(osb_s.astype(BF16), ROW_PAD), _pad_rows(oh_s, ROW_PAD),
        cache_mem_k[0].reshape(DB, M, D).astype(BF16), cache_mem_v[0].reshape(DB, M, D).astype(BF16),
        w, no_counts, DB, ROW_PAD, LS)
    unpad = lambda a: a.reshape(DB, ROW_PAD, a.shape[-1])[:, :LS].reshape(Ts, a.shape[-1])
    h2_s, h2b_s, idx_s, gate_s, rank_s = (unpad(h2_s), unpad(h2b_s), unpad(idx_s), unpad(gate_s),
                                          unpad(rank_s))

    wgu_b, wd_b = w_gate_up[0].astype(BF16), w_down[0].astype(BF16)
    bgu, bd = b_gate_up[0][:, None, :], b_down[0][:, None, :]
    g3, b3 = row1(ln3_g[0]), row1(ln3_b[0])

    def moe(h2, h2b, idx, rank, cnt, gate, bm):
        n_tok = h2.shape[0]
        slot, tok_of_slot, block_e = _routing(idx[:, :TOP_K], rank[:, :TOP_K],
                                              cnt[0, :N_EXPERTS].astype(jnp.int32), bm)
        xg = h2b.at[tok_of_slot].get(mode="promise_in_bounds")
        eo = _experts(xg, block_e, wgu_b, bgu, wd_b, bd, bm)
        ffg = eo.at[slot.T.reshape(-1)].get(mode="promise_in_bounds").reshape(TOP_K, n_tok, D)
        return _combine(h2, ffg, gate, g3, b3, 0)

    y_p = moe(h2_p, h2b_p, idx_p, rank_p, cnt_p, gate_p, MOE_BM)
    y_s = moe(h2_s, h2b_s, idx_s, rank_s, cnt_s, gate_s, MOE_BM_SAMPLE)

    sbshape = lambda a, b_, l_: a.reshape(1, b_, l_, SB_HEADS, SB_DH)
    sbshape_t = lambda a: a.reshape(1, B, SB_HEADS, SB_DH, L).transpose(0, 1, 4, 2, 3)
    memshape = lambda a: a.reshape(1, B, M, MEM_HEADS, MEM_DH)
    return (y_p.reshape(B, L, D), y_s.reshape(DB, LS, D),
            sbshape_t(kt_p), sbshape_t(vt32_p), s_p[None],
            memshape(mk_p), memshape(mv_p),
            sbshape(k_s, DB, LS), sbshape(v_s, DB, LS), s_s[None])
```
